```python
import math
import jax, jax.numpy as jnp
from jax import lax
import numpy as np


D_MODEL = 1024
BATCH = 32
SEQ = 2048
DEPTH = 2

CTX_LEN = 256
GRID_W = 64
N_MIXERS = 2
N_GLA = (DEPTH + N_MIXERS - 1) // N_MIXERS
N_MLA = DEPTH // N_MIXERS
ALPHA = (2.0 * DEPTH) ** 0.25
BETA = (8.0 * DEPTH) ** -0.25
LN_EPS = 1e-5
RMS_EPS = 1e-6

GLA_HEADS = 4
GLA_DK = D_MODEL // 2 // GLA_HEADS
GLA_DV = D_MODEL // GLA_HEADS
GLA_GATE_RANK = 16
GLA_TAU = 16.0
GLA_CHUNK = 64

MLA_HEADS = D_MODEL // 128
MLA_NOPE = 128
MLA_ROPE = 64
MLA_V = 128
MLA_Q_RANK = 256
MLA_KV_RANK = 128
ATTN_BLOCK = 128
ROPE_BASE = 10000.0

PEER_HEADS = 8
PEER_NKEYS = 128
PEER_EXPERTS = PEER_NKEYS * PEER_NKEYS
PEER_QDIM = 256
PEER_TOPK = 16
PEER_BLOCK = 128

kernel_name = 'hybrid_gla_mla_peer_diffusion_block'


def layer_norm(x, g, b):
    xf = x.astype(jnp.float32)
    mu = jnp.mean(xf, axis=-1, keepdims=True)
    var = jnp.mean(jnp.square(xf - mu), axis=-1, keepdims=True)
    return ((xf - mu) * lax.rsqrt(var + LN_EPS)).astype(x.dtype) * g + b


def rms_norm(x, g):
    xf = x.astype(jnp.float32)
    return (xf * lax.rsqrt(jnp.mean(xf * xf, axis=-1, keepdims=True) + RMS_EPS)).astype(x.dtype) * g


def modulate(x, shift, scale):
    return x * (1 + scale) + shift


def axial_rope_tables(n_tokens):
    rows = n_tokens // GRID_W
    row = jnp.repeat(jnp.arange(rows, dtype=jnp.float32), GRID_W)
    col = jnp.tile(jnp.arange(GRID_W, dtype=jnp.float32), rows)
    n_freq = MLA_ROPE // 4
    inv_freq = ROPE_BASE ** (-jnp.arange(n_freq, dtype=jnp.float32) / n_freq)
    ang = jnp.stack([row[:, None] * inv_freq, col[:, None] * inv_freq], axis=1)
    return jnp.cos(ang), jnp.sin(ang)


def rope_2d(x, cos, sin):
    xs = x.reshape(x.shape[:-1] + (2, 2, MLA_ROPE // 4))
    cos = cos.astype(x.dtype)
    sin = sin.astype(x.dtype)
    x1, x2 = xs[..., 0, :], xs[..., 1, :]
    out = jnp.stack([x1 * cos - x2 * sin, x1 * sin + x2 * cos], axis=-2)
    return out.reshape(x.shape)


def gla_chunked(q, k, v, logg, s0, strict):
    b_, h_, l_, dk = q.shape
    dv = v.shape[-1]
    n = l_ // GLA_CHUNK
    rs = lambda t: t.reshape(b_, h_, n, GLA_CHUNK, t.shape[-1])
    q, k, v, logg = rs(q), rs(k), rs(v), rs(logg)
    cum = jnp.cumsum(logg.astype(jnp.float32), axis=3)
    last = cum[..., -1:, :]
    qe = q * jnp.exp(cum).astype(q.dtype)
    ke = k * jnp.exp(-cum).astype(k.dtype)
    kd = k * jnp.exp(last - cum).astype(k.dtype)
    i = jnp.arange(GLA_CHUNK)
    mask = (i[:, None] > i[None, :]) if strict else (i[:, None] >= i[None, :])
    a = jnp.where(mask, jnp.einsum('bhnid,bhnjd->bhnij', qe, ke), 0)
    intra = jnp.einsum('bhnij,bhnjv->bhniv', a, v)
    chunk_kv = jnp.einsum('bhnjd,bhnjv->bhndv', kd, v)
    decay = jnp.exp(last[..., 0, :]).astype(v.dtype)

    def step(s, xs):
        qe_n, kv_n, dec_n = xs
        out = jnp.einsum('bhid,bhdv->bhiv', qe_n, s)
        return dec_n[..., None] * s + kv_n, out

    s_final, inter = lax.scan(step, s0, (jnp.moveaxis(qe, 2, 0), jnp.moveaxis(chunk_kv, 2, 0), jnp.moveaxis(decay, 2, 0)))
    o = intra + jnp.moveaxis(inter, 0, 2)
    return o.reshape(b_, h_, l_, dv), s_final


def gla_mixer(x_lat, x_ctx, w_in, gf_a, gf_b, gf_bias, gb_a, gb_b, gb_bias, norm_g, w_out, need_ctx):
    hk, hv = GLA_HEADS * GLA_DK, GLA_HEADS * GLA_DV

    def heads(t, d):
        b_, l_, _ = t.shape
        return t.reshape(b_, l_, GLA_HEADS, d).transpose(0, 2, 1, 3)

    def project(x):
        q, k, v, r = jnp.split(x @ w_in, [hk, 2 * hk, 2 * hk + hv], axis=-1)
        lf = jax.nn.log_sigmoid(x @ gf_a @ gf_b + gf_bias) / GLA_TAU
        lb = jax.nn.log_sigmoid(x @ gb_a @ gb_b + gb_bias) / GLA_TAU
        return (heads(q, GLA_DK) * GLA_DK ** -0.5, heads(k, GLA_DK), heads(v, GLA_DV), r,
                heads(lf, GLA_DK), heads(lb, GLA_DK))

    ql, kl, vl, rl, lfl, lbl = project(x_lat)
    qc, kc, vc, rc, lfc, lbc = project(x_ctx)
    flip = lambda t: jnp.flip(t, axis=2)
    s0 = jnp.zeros(qc.shape[:2] + (GLA_DK, GLA_DV), qc.dtype)
    oc_f, sc_f = gla_chunked(qc, kc, vc, lfc, s0, False)
    oc_b, sc_b = gla_chunked(flip(qc), flip(kc), flip(vc), flip(lbc), s0, True)
    ol_f, _ = gla_chunked(ql, kl, vl, lfl, sc_f, False)
    ol_b, _ = gla_chunked(flip(ql), flip(kl), flip(vl), flip(lbl), sc_b, True)

    def output(o, r):
        o = rms_norm(o, norm_g)
        b_, h_, l_, dv = o.shape
        o = o.transpose(0, 2, 1, 3).reshape(b_, l_, h_ * dv)
        return (o * jax.nn.silu(r)) @ w_out

    out_lat = output(ol_f + flip(ol_b), rl)
    out_ctx = output(oc_f + flip(oc_b), rc) if need_ctx else None
    return out_lat, out_ctx


def softmax_attend(q, k, v):
    s = jnp.einsum('bqhd,bkhd->bhqk', q, k) * (MLA_NOPE + MLA_ROPE) ** -0.5
    p = jax.nn.softmax(s.astype(jnp.float32), axis=-1).astype(v.dtype)
    return jnp.einsum('bhqk,bkhd->bqhd', p, v)


def mla_mixer(x_lat, x_ctx, cos, sin, w_down, q_norm_g, kv_norm_g, w_uq, w_ukv, w_out, need_ctx):
    def project(x, rotate):
        b_, l_, _ = x.shape
        cq, ckv, kr = jnp.split(x @ w_down, [MLA_Q_RANK, MLA_Q_RANK + MLA_KV_RANK], axis=-1)
        q = (rms_norm(cq, q_norm_g) @ w_uq).reshape(b_, l_, MLA_HEADS, MLA_NOPE + MLA_ROPE)
        kv = (rms_norm(ckv, kv_norm_g) @ w_ukv).reshape(b_, l_, MLA_HEADS, MLA_NOPE + MLA_V)
        q_nope, q_rope = q[..., :MLA_NOPE], q[..., MLA_NOPE:]
        k_nope, v = kv[..., :MLA_NOPE], kv[..., MLA_NOPE:]
        if rotate:
            q_rope = rope_2d(q_rope, cos[:, None], sin[:, None])
            kr = rope_2d(kr, cos, sin)
        k_rope = jnp.broadcast_to(kr[:, :, None, :], (b_, l_, MLA_HEADS, MLA_ROPE))
        return jnp.concatenate([q_nope, q_rope], -1), jnp.concatenate([k_nope, k_rope], -1), v

    q_l, k_l, v_l = project(x_lat, True)
    q_c, k_c, v_c = project(x_ctx, False)
    b_, l_ = x_lat.shape[:2]
    k_all = jnp.concatenate([k_c, k_l], axis=1)
    v_all = jnp.concatenate([v_c, v_l], axis=1)
    n_blk = l_ // ATTN_BLOCK
    q_blocks = q_l.reshape(b_, n_blk, ATTN_BLOCK, MLA_HEADS, MLA_NOPE + MLA_ROPE).swapaxes(0, 1)
    o_l = lax.map(lambda qb: softmax_attend(qb, k_all, v_all), q_blocks)
    o_l = o_l.swapaxes(0, 1).reshape(b_, l_, MLA_HEADS * MLA_V)
    out_lat = o_l @ w_out
    if need_ctx:
        o_c = softmax_attend(q_c, k_c, v_c)
        out_ctx = o_c.reshape(o_c.shape[0], o_c.shape[1], MLA_HEADS * MLA_V) @ w_out
    else:
        out_ctx = None
    return out_lat, out_ctx


def peer(x, w_query, keys_a, keys_b, expert_u, expert_v):
    t = x.shape[0]
    xb = x.reshape(t // PEER_BLOCK, PEER_BLOCK, D_MODEL)

    def block(xt):
        q = (xt @ w_query).reshape(PEER_BLOCK, PEER_HEADS, 2, PEER_QDIM // 2)
        s_a = jnp.einsum('thd,hkd->thk', q[:, :, 0], keys_a)
        s_b = jnp.einsum('thd,hkd->thk', q[:, :, 1], keys_b)
        va, ia = lax.top_k(s_a, PEER_TOPK)
        vb, ib = lax.top_k(s_b, PEER_TOPK)
        cand = (va[..., :, None] + vb[..., None, :]).reshape(PEER_BLOCK, PEER_HEADS, PEER_TOPK * PEER_TOPK)
        vals, pos = lax.top_k(cand, PEER_TOPK)
        idx = (jnp.take_along_axis(ia, pos // PEER_TOPK, axis=-1) * PEER_NKEYS
               + jnp.take_along_axis(ib, pos % PEER_TOPK, axis=-1))
        g = jax.nn.softmax(vals.astype(jnp.float32), axis=-1).astype(xt.dtype)
        idx = idx.reshape(PEER_BLOCK, PEER_HEADS * PEER_TOPK)
        h = jax.nn.gelu(jnp.einsum('td,ted->te', xt, expert_u[idx]), approximate=False)
        w = g.reshape(PEER_BLOCK, PEER_HEADS * PEER_TOPK) * h
        return jnp.einsum('te,ted->td', w, expert_v[idx])

    return lax.map(block, xb).reshape(t, D_MODEL)


def setup_inputs(seed: int = 0) -> dict:
    key = jax.random.key(seed)
    ks = jax.random.split(key, 32)
    nrm = lambda k, shape, s: jax.random.normal(k, shape, jnp.float32) * s
    hk, hv = GLA_HEADS * GLA_DK, GLA_HEADS * GLA_DV
    d = D_MODEL
    return {
        'x': nrm(ks[0], (BATCH, SEQ, d), 1.0),
        'c': nrm(ks[1], (BATCH, d), 1.0),
        'ctx': nrm(ks[2], (BATCH, CTX_LEN, d), 1.0),
        'c_ctx': nrm(ks[3], (d,), 1.0),
        'ada_w': nrm(ks[4], (DEPTH, d, 6 * d), 0.5 * d ** -0.5),
        'ada_b': nrm(ks[5], (DEPTH, 6 * d), 0.01),
        'ln_tm_g': 1.0 + nrm(ks[6], (DEPTH, d), 0.02),
        'ln_tm_b': nrm(ks[7], (DEPTH, d), 0.01),
        'ln_cm_g': 1.0 + nrm(ks[8], (DEPTH, d), 0.02),
        'ln_cm_b': nrm(ks[9], (DEPTH, d), 0.01),
        'gla_w_in': nrm(ks[10], (N_GLA, d, 2 * hk + 2 * hv), d ** -0.5),
        'gla_gate_fwd_a': nrm(ks[11], (N_GLA, d, GLA_GATE_RANK), d ** -0.5),
        'gla_gate_fwd_b': nrm(ks[12], (N_GLA, GLA_GATE_RANK, hk), GLA_GATE_RANK ** -0.5),
        'gla_gate_fwd_bias': nrm(ks[13], (N_GLA, hk), 0.1),
        'gla_gate_bwd_a': nrm(ks[14], (N_GLA, d, GLA_GATE_RANK), d ** -0.5),
        'gla_gate_bwd_b': nrm(ks[15], (N_GLA, GLA_GATE_RANK, hk), GLA_GATE_RANK ** -0.5),
        'gla_gate_bwd_bias': nrm(ks[16], (N_GLA, hk), 0.1),
        'gla_norm_g': 1.0 + nrm(ks[17], (N_GLA, GLA_DV), 0.02),
        'gla_w_out': nrm(ks[18], (N_GLA, hv, d), BETA * hv ** -0.5),
        'mla_w_down': nrm(ks[19], (N_MLA, d, MLA_Q_RANK + MLA_KV_RANK + MLA_ROPE), d ** -0.5),
        'mla_q_norm_g': 1.0 + nrm(ks[20], (N_MLA, MLA_Q_RANK), 0.02),
        'mla_kv_norm_g': 1.0 + nrm(ks[21], (N_MLA, MLA_KV_RANK), 0.02),
        'mla_w_uq': nrm(ks[22], (N_MLA, MLA_Q_RANK, MLA_HEADS * (MLA_NOPE + MLA_ROPE)), MLA_Q_RANK ** -0.5),
        'mla_w_ukv': nrm(ks[23], (N_MLA, MLA_KV_RANK, MLA_HEADS * (MLA_NOPE + MLA_V)), MLA_KV_RANK ** -0.5),
        'mla_w_out': nrm(ks[24], (N_MLA, MLA_HEADS * MLA_V, d), BETA * (MLA_HEADS * MLA_V) ** -0.5),
        'peer_w_query': nrm(ks[25], (DEPTH, d, PEER_HEADS * PEER_QDIM), d ** -0.5),
        'peer_keys_a': nrm(ks[26], (DEPTH, PEER_HEADS, PEER_NKEYS, PEER_QDIM // 2), (PEER_QDIM // 2) ** -0.5),
        'peer_keys_b': nrm(ks[27], (DEPTH, PEER_HEADS, PEER_NKEYS, PEER_QDIM // 2), (PEER_QDIM // 2) ** -0.5),
        'peer_u': nrm(ks[28], (DEPTH, PEER_EXPERTS, d), d ** -0.5),
        'peer_v': nrm(ks[29], (DEPTH, PEER_EXPERTS, d), BETA),
    }


def reference(x, c, ctx, c_ctx, ada_w, ada_b, ln_tm_g, ln_tm_b, ln_cm_g, ln_cm_b,
              gla_w_in, gla_gate_fwd_a, gla_gate_fwd_b, gla_gate_fwd_bias,
              gla_gate_bwd_a, gla_gate_bwd_b, gla_gate_bwd_bias, gla_norm_g, gla_w_out,
              mla_w_down, mla_q_norm_g, mla_kv_norm_g, mla_w_uq, mla_w_ukv, mla_w_out,
              peer_w_query, peer_keys_a, peer_keys_b, peer_u, peer_v):
    b_, l_, d = x.shape
    cos, sin = axial_rope_tables(l_)
    h_lat, h_ctx = x, ctx
    sc_silu = jax.nn.silu(c)
    cc_silu = jax.nn.silu(c_ctx)
    for i in range(DEPTH):
        last = i == DEPTH - 1
        mod = sc_silu @ ada_w[i] + ada_b[i]
        mod_c = cc_silu @ ada_w[i] + ada_b[i]
        sh_t, sc_t, g_t, sh_c, sc_c, g_c = jnp.split(mod[:, None, :], 6, axis=-1)
        csh_t, csc_t, cg_t, csh_c, csc_c, cg_c = jnp.split(mod_c, 6, axis=-1)
        xl = modulate(h_lat, sh_t, sc_t)
        xc = modulate(h_ctx, csh_t, csc_t)
        j = i // N_MIXERS
        if i % N_MIXERS == 0:
            out_l, out_c = gla_mixer(xl, xc, gla_w_in[j], gla_gate_fwd_a[j], gla_gate_fwd_b[j], gla_gate_fwd_bias[j],
                                     gla_gate_bwd_a[j], gla_gate_bwd_b[j], gla_gate_bwd_bias[j],
                                     gla_norm_g[j], gla_w_out[j], not last)
        else:
            out_l, out_c = mla_mixer(xl, xc, cos, sin, mla_w_down[j], mla_q_norm_g[j], mla_kv_norm_g[j],
                                     mla_w_uq[j], mla_w_ukv[j], mla_w_out[j], not last)
        h_lat = layer_norm(ALPHA * h_lat + g_t * out_l, ln_tm_g[i], ln_tm_b[i])
        xl = modulate(h_lat, sh_c, sc_c).reshape(b_ * l_, d)
        if not last:
            h_ctx = layer_norm(ALPHA * h_ctx + cg_t * out_c, ln_tm_g[i], ln_tm_b[i])
            xc = modulate(h_ctx, csh_c, csc_c).reshape(-1, d)
            y = peer(jnp.concatenate([xl, xc], axis=0), peer_w_query[i], peer_keys_a[i], peer_keys_b[i], peer_u[i], peer_v[i])
            y_l, y_c = y[:b_ * l_], y[b_ * l_:]
            h_ctx = layer_norm(ALPHA * h_ctx + cg_c * y_c.reshape(h_ctx.shape), ln_cm_g[i], ln_cm_b[i])
        else:
            y_l = peer(xl, peer_w_query[i], peer_keys_a[i], peer_keys_b[i], peer_u[i], peer_v[i])
        h_lat = layer_norm(ALPHA * h_lat + g_c * y_l.reshape(b_, l_, d), ln_cm_g[i], ln_cm_b[i])
    return h_lat
```

```python
import functools
import math

import jax
import jax.numpy as jnp
from jax import lax
from jax.experimental import pallas as pl
from jax.experimental.pallas import tpu as pltpu

F32 = jnp.float32
BF16 = jnp.bfloat16
NEG_INF = float("-inf")

DEPTH = 2
ALPHA = (2.0 * DEPTH) ** 0.25
LN_EPS = 1e-5
RMS_EPS = 1e-6

GLA_HEADS = 4
GLA_TAU = 16.0
GLA_CHUNK = 64
GLA_GATE_RANK = 16

MLA_HEADS = 8
MLA_NOPE = 128
MLA_ROPE = 64
MLA_V = 128
MLA_Q_RANK = 256
MLA_KV_RANK = 128
MLA_QK_PAD = 256
GRID_W = 64
ROPE_BASE = 10000.0

PEER_HEADS = 8
PEER_NKEYS = 128
PEER_TOPK = 16
PEER_NEXT = PEER_TOPK + 1

LANES = 128
SUBLANES = 8
VMEM_LIMIT = 56 * 1024 * 1024


def _cparams(sem):
    return pltpu.CompilerParams(dimension_semantics=sem, vmem_limit_bytes=VMEM_LIMIT)


def _layer_norm(z, g, b):
    mu = jnp.mean(z, axis=-1, keepdims=True)
    zc = z - mu
    var = jnp.mean(zc * zc, axis=-1, keepdims=True)
    return zc * lax.rsqrt(var + LN_EPS) * g + b


def _silu(x):
    return x / (1.0 + jnp.exp(-x))


def _tile(n, pref):
    t = pref
    while n % t:
        t //= 2
    return t


def _ada_kernel(c_ref, w_ref, b_ref, o_ref):
    s = _silu(c_ref[...])
    o_ref[0] = jnp.dot(s, w_ref[0], preferred_element_type=F32,
                       precision=lax.Precision.HIGHEST) + b_ref[0]


def _ada(c_all, ada_w, ada_b):
    nb, d = c_all.shape
    depth, _, n6 = ada_w.shape
    tn = 1024
    return pl.pallas_call(
        _ada_kernel,
        grid=(depth, n6 // tn),
        in_specs=[pl.BlockSpec((nb, d), lambda i, j: (0, 0)),
                  pl.BlockSpec((1, d, tn), lambda i, j: (i, 0, j)),
                  pl.BlockSpec((1, 1, tn), lambda i, j: (i, 0, j))],
        out_specs=pl.BlockSpec((1, nb, tn), lambda i, j: (i, 0, j)),
        out_shape=jax.ShapeDtypeStruct((depth, nb, n6), F32),
        compiler_params=_cparams(("arbitrary", "arbitrary")),
        name="ada_mod",
    )(c_all, ada_w, ada_b.reshape(depth, 1, n6))


def _gla_inproj_kernel(h_ref, sh_ref, sc_ref, w_ref, ga_ref, gb_ref, gbias_ref,
                       q_ref, k_ref, v_ref, r_ref, lf_ref, lb_ref, *, hk, hv, dk):
    x = h_ref[...] * (1.0 + sc_ref[0]) + sh_ref[0]
    xb = x.astype(BF16)
    y = jnp.dot(xb, w_ref[...], preferred_element_type=F32)
    q_ref[...] = y[:, :hk] * (dk ** -0.5)
    k_ref[...] = y[:, hk:2 * hk]
    v_ref[...] = y[:, 2 * hk:2 * hk + hv]
    r_ref[...] = y[:, 2 * hk + hv:]
    g1 = jnp.dot(xb, ga_ref[...], preferred_element_type=F32)
    z = jnp.dot(g1.astype(BF16), gb_ref[...], preferred_element_type=F32) + gbias_ref[...]
    ls = (jnp.minimum(z, 0.0) - jnp.log1p(jnp.exp(-jnp.abs(z)))) * (1.0 / GLA_TAU)
    lf_ref[...] = ls[:, :hk]
    lb_ref[...] = ls[:, hk:]


def _gla_inproj(h2, sh, sc, mod_idx, w_in, ga, gb, gbias, hk, hv):
    t, d = h2.shape
    tm = _tile(t, 256)
    row = lambda i: (i, 0)
    const = lambda i: (0, 0)
    kern = functools.partial(_gla_inproj_kernel, hk=hk, hv=hv, dk=hk // GLA_HEADS)
    return pl.pallas_call(
        kern,
        grid=(t // tm,),
        in_specs=[pl.BlockSpec((tm, d), row),
                  pl.BlockSpec((1, 1, d), lambda i: (mod_idx(i, tm), 0, 0)),
                  pl.BlockSpec((1, 1, d), lambda i: (mod_idx(i, tm), 0, 0)),
                  pl.BlockSpec(w_in.shape, const),
                  pl.BlockSpec(ga.shape, const),
                  pl.BlockSpec(gb.shape, const),
                  pl.BlockSpec(gbias.shape, const)],
        out_specs=[pl.BlockSpec((tm, hk), row), pl.BlockSpec((tm, hk), row),
                   pl.BlockSpec((tm, hv), row), pl.BlockSpec((tm, hv), row),
                   pl.BlockSpec((tm, hk), row), pl.BlockSpec((tm, hk), row)],
        out_shape=[jax.ShapeDtypeStruct((t, hk), F32), jax.ShapeDtypeStruct((t, hk), F32),
                   jax.ShapeDtypeStruct((t, hv), F32), jax.ShapeDtypeStruct((t, hv), F32),
                   jax.ShapeDtypeStruct((t, hk), F32), jax.ShapeDtypeStruct((t, hk), F32)],
        compiler_params=_cparams(("arbitrary",)),
        name="gla_inproj",
    )(h2, sh, sc, w_in, ga, gb, gbias)


def _split3(x):
    x1 = x.astype(BF16)
    r1 = x - x1.astype(F32)
    x2 = r1.astype(BF16)
    x3 = (r1 - x2.astype(F32)).astype(BF16)
    return x1, x2, x3


def _gla_chunk(q, k, v, lg, st_ref, tri, mask, last_row):
    l1, l2, l3 = _split3(lg)
    cum = (jnp.dot(tri, l1, preferred_element_type=F32)
           + jnp.dot(tri, l2, preferred_element_type=F32)
           + jnp.dot(tri, l3, preferred_element_type=F32))
    last = cum[last_row:last_row + 1, :]
    qe = (q * jnp.exp(cum)).astype(BF16)
    ke = (k * jnp.exp(-cum)).astype(BF16)
    kd = (k * jnp.exp(last - cum)).astype(BF16)
    vb = v.astype(BF16)
    a = lax.dot_general(qe, ke, (((1,), (1,)), ((), ())), preferred_element_type=F32)
    a = jnp.where(mask, a, 0.0)
    st = st_ref[...]
    o = jnp.dot(a.astype(BF16), vb, preferred_element_type=F32)
    o = o + lax.dot_general(qe, st.astype(BF16), (((1,), (1,)), ((), ())),
                            preferred_element_type=F32)
    kvt = lax.dot_general(vb, kd, (((0,), (0,)), ((), ())), preferred_element_type=F32)
    st_ref[...] = st * jnp.exp(last) + kvt
    return o


def _gla_scan_kernel(qc_ref, kc_ref, vc_ref, lfc_ref, lbc_ref,
                     ql_ref, kl_ref, vl_ref, lfl_ref, lbl_ref, ng_ref,
                     oc_ref, ol_ref, stf_ref, stb_ref):
    c = GLA_CHUNK
    ii = lax.broadcasted_iota(jnp.int32, (c, c), 0)
    jj = lax.broadcasted_iota(jnp.int32, (c, c), 1)
    mask_f = ii >= jj
    mask_b = ii < jj
    tri_f = jnp.where(mask_f, 1.0, 0.0).astype(BF16)
    tri_b = jnp.where(ii <= jj, 1.0, 0.0).astype(BF16)

    stf_ref[...] = jnp.zeros_like(stf_ref)
    stb_ref[...] = jnp.zeros_like(stb_ref)
    oc_ref[...] = jnp.zeros_like(oc_ref)
    ol_ref[...] = jnp.zeros_like(ol_ref)

    def run(q_ref, k_ref, v_ref, lf_ref, lb_ref, o_ref):
        n = q_ref.shape[1] // c

        def body(i, carry):
            sf = pl.ds(pl.multiple_of(i * c, c), c)
            sb = pl.ds(pl.multiple_of((n - 1 - i) * c, c), c)
            of = _gla_chunk(q_ref[0, sf, :], k_ref[0, sf, :], v_ref[0, sf, :], lf_ref[0, sf, :],
                            stf_ref, tri_f, mask_f, c - 1)
            o_ref[0, sf, :] += of
            ob = _gla_chunk(q_ref[0, sb, :], k_ref[0, sb, :], v_ref[0, sb, :], lb_ref[0, sb, :],
                            stb_ref, tri_b, mask_b, 0)
            o_ref[0, sb, :] += ob
            return carry

        lax.fori_loop(0, n, body, 0)

    run(qc_ref, kc_ref, vc_ref, lfc_ref, lbc_ref, oc_ref)
    run(ql_ref, kl_ref, vl_ref, lfl_ref, lbl_ref, ol_ref)

    g = ng_ref[...]

    def norm(o_ref):
        rows = o_ref.shape[1]
        blk = _tile(rows, 256)

        def body(i, carry):
            s = pl.ds(pl.multiple_of(i * blk, blk), blk)
            o = o_ref[0, s, :]
            o_ref[0, s, :] = o * lax.rsqrt(jnp.mean(o * o, axis=-1, keepdims=True) + RMS_EPS) * g
            return carry

        lax.fori_loop(0, rows // blk, body, 0)

    norm(oc_ref)
    norm(ol_ref)


def _gla_scan(proj_c, proj_l, norm_g, b, lc, ll, hk, hv):
    dk, dv = hk // GLA_HEADS, hv // GLA_HEADS
    qc, kc, vc, _, lfc, lbc = [a.reshape(b, lc, -1) for a in proj_c]
    ql, kl, vl, _, lfl, lbl = [a.reshape(b, ll, -1) for a in proj_l]
    head = lambda i, h: (i, 0, h)
    specs = []
    for n in (lc, ll):
        specs += [pl.BlockSpec((1, n, dk), head), pl.BlockSpec((1, n, dk), head),
                  pl.BlockSpec((1, n, dv), head), pl.BlockSpec((1, n, dk), head),
                  pl.BlockSpec((1, n, dk), head)]
    specs.append(pl.BlockSpec((1, dv), lambda i, h: (0, 0)))
    oc, ol = pl.pallas_call(
        _gla_scan_kernel,
        grid=(b, GLA_HEADS),
        in_specs=specs,
        out_specs=[pl.BlockSpec((1, lc, dv), head), pl.BlockSpec((1, ll, dv), head)],
        out_shape=[jax.ShapeDtypeStruct((b, lc, hv), F32), jax.ShapeDtypeStruct((b, ll, hv), F32)],
        scratch_shapes=[pltpu.VMEM((dv, dk), F32), pltpu.VMEM((dv, dk), F32)],
        compiler_params=_cparams(("arbitrary", "arbitrary")),
        name="gla_scan",
    )(qc, kc, vc, lfc, lbc, ql, kl, vl, lfl, lbl, norm_g.reshape(1, dv))
    return oc, ol


def _mix_out_kernel(*refs, gated):
    if gated:
        o_ref, r_ref, h_ref, g_ref, w_ref, lng_ref, lnb_ref, out_ref = refs
        o = o_ref[...] * _silu(r_ref[...])
    else:
        o_ref, h_ref, g_ref, w_ref, lng_ref, lnb_ref, out_ref = refs
        o = o_ref[...]
    y = jnp.dot(o.astype(BF16), w_ref[...], preferred_element_type=F32)
    z = ALPHA * h_ref[...] + g_ref[0] * y
    out_ref[...] = _layer_norm(z, lng_ref[...], lnb_ref[...])


def _mix_out(o2, r2, h2, gate, mod_idx, w_out, ln_g, ln_b):
    t, d = h2.shape
    kdim = o2.shape[1]
    tm = _tile(t, 512)
    row = lambda i: (i, 0)
    const = lambda i: (0, 0)
    gated = r2 is not None
    ins = [o2] + ([r2] if gated else []) + [h2, gate, w_out, ln_g.reshape(1, d), ln_b.reshape(1, d)]
    specs = [pl.BlockSpec((tm, kdim), row)] + ([pl.BlockSpec((tm, kdim), row)] if gated else [])
    specs += [pl.BlockSpec((tm, d), row),
              pl.BlockSpec((1, 1, d), lambda i: (mod_idx(i, tm), 0, 0)),
              pl.BlockSpec(w_out.shape, const),
              pl.BlockSpec((1, d), const), pl.BlockSpec((1, d), const)]
    return pl.pallas_call(
        functools.partial(_mix_out_kernel, gated=gated),
        grid=(t // tm,),
        in_specs=specs,
        out_specs=pl.BlockSpec((tm, d), row),
        out_shape=jax.ShapeDtypeStruct((t, d), F32),
        compiler_params=_cparams(("arbitrary",)),
        name="mix_out",
    )(*ins)


def _top_rows(s, n, dst_ref):
    cur = s
    for k in range(n):
        m = jnp.max(cur, axis=0, keepdims=True)
        dst_ref[k:k + 1, :] = m
        cur = jnp.where(cur >= m, NEG_INF, cur)


def _peer_route(sa, sb, va_ref, vb_ref):
    n = PEER_NEXT
    _top_rows(sa, n, va_ref)
    _top_rows(sb, n, vb_ref)
    va = va_ref[...]
    vb = vb_ref[...]
    rows = lax.broadcasted_iota(jnp.int32, (SUBLANES, LANES), 0)
    blocks = [va + vb[0:1, :]]
    for j in range(1, SUBLANES):
        cnt = n // (j + 1)
        blk = va[0:SUBLANES, :] + vb[j:j + 1, :]
        blocks.append(blk if cnt >= SUBLANES else jnp.where(rows < cnt, blk, NEG_INF))
    blocks.append(vb[SUBLANES:, :] + va[0:1, :])
    cand = jnp.concatenate(blocks, axis=0)
    e = []
    for k in range(n):
        m = jnp.max(cand, axis=0, keepdims=True)
        e.append(m)
        cand = jnp.where(cand >= m, NEG_INF, cand)
    tau = 0.5 * (e[PEER_TOPK - 1] + e[PEER_TOPK])
    z = jnp.ones_like(e[0])
    for k in range(1, PEER_TOPK):
        z = z + jnp.exp(e[k] - e[0])
    va0 = va[0:1, :]
    vb0 = vb[0:1, :]
    ea = jnp.exp(sa - va0) / z
    eb = jnp.exp(sb - vb0)
    th = jnp.exp(tau - vb0 - sa)
    return ea, eb, th


def _peer_kernel(h_ref, sh_ref, sc_ref, gate_ref, wq_ref, ka_ref, kb_ref, u_ref, vt_ref,
                 lng_ref, lnb_ref, out_ref,
                 xt_s, qt_s, ea_s, eb_s, th_s, ht_s, p_s, yt_s, va_s, vb_s, *, a_per_chunk):
    j = pl.program_id(1)
    tm = h_ref.shape[0]
    nlt = tm // LANES
    nk = PEER_NKEYS

    @pl.when(j == 0)
    def _prologue():
        x = h_ref[...] * (1.0 + sc_ref[0]) + sh_ref[0]
        xt_s[...] = x.T.astype(BF16)
        qt_s[...] = jnp.dot(wq_ref[...], xt_s[...], preferred_element_type=F32)
        va_s[...] = jnp.full(va_s.shape, NEG_INF, F32)
        vb_s[...] = jnp.full(vb_s.shape, NEG_INF, F32)

        def head_body(hd, carry):
            ra = pl.ds(pl.multiple_of(hd * 2 * nk, nk), nk)
            rb = pl.ds(pl.multiple_of(hd * 2 * nk + nk, nk), nk)
            sa = jnp.dot(ka_ref[hd], qt_s[ra, :].astype(BF16), preferred_element_type=F32)
            sb = jnp.dot(kb_ref[hd], qt_s[rb, :].astype(BF16), preferred_element_type=F32)
            for lt in range(nlt):
                ls = slice(lt * LANES, (lt + 1) * LANES)
                ea, eb, th = _peer_route(sa[:, ls], sb[:, ls], va_s, vb_s)
                ea_s[hd, :, ls] = ea
                eb_s[hd, :, ls] = eb
                th_s[hd, :, ls] = th
            return carry

        lax.fori_loop(0, PEER_HEADS, head_body, 0)
        yt_s[...] = jnp.zeros_like(yt_s)

    ht_s[...] = jnp.dot(u_ref[...], xt_s[...], preferred_element_type=F32)

    arows = pl.ds(pl.multiple_of(j * a_per_chunk, a_per_chunk), a_per_chunk)

    def lane_body(lt, carry):
        ls = pl.ds(pl.multiple_of(lt * LANES, LANES), LANES)
        ths = [th_s[hd, arows, ls] for hd in range(PEER_HEADS)]
        eas = [ea_s[hd, arows, ls] for hd in range(PEER_HEADS)]
        for al in range(a_per_chunk):
            r = slice(al * nk, (al + 1) * nk)
            w = jnp.zeros((nk, LANES), F32)
            for hd in range(PEER_HEADS):
                ebv = eb_s[hd, :, ls]
                w = w + jnp.where(ebv >= ths[hd][al:al + 1, :], ebv, 0.0) * eas[hd][al:al + 1, :]
            hh = ht_s[r, ls]
            gl = 0.5 * hh * (1.0 + lax.erf(hh * (2.0 ** -0.5)))
            p_s[r, ls] = (w * gl).astype(BF16)
        return carry

    lax.fori_loop(0, nlt, lane_body, 0)
    yt_s[...] += jnp.dot(vt_ref[...], p_s[...], preferred_element_type=F32)

    @pl.when(j == pl.num_programs(1) - 1)
    def _epilogue():
        y = yt_s[...].T
        z = ALPHA * h_ref[...] + gate_ref[0] * y
        out_ref[...] = _layer_norm(z, lng_ref[...], lnb_ref[...])


def _peer(h2, sh, sc, gate, mod_idx, wq_t, ka, kb, u_b, vt_b, ln_g, ln_b):
    t, d = h2.shape
    ne = u_b.shape[0]
    tm = _tile(t, 512)
    a_per_chunk = 8
    ce = a_per_chunk * PEER_NKEYS
    row = lambda i, j: (i, 0)
    const2 = lambda i, j: (0, 0)
    const3 = lambda i, j: (0, 0, 0)
    mod = lambda i, j: (mod_idx(i, tm), 0, 0)
    return pl.pallas_call(
        functools.partial(_peer_kernel, a_per_chunk=a_per_chunk),
        grid=(t // tm, ne // ce),
        in_specs=[pl.BlockSpec((tm, d), row),
                  pl.BlockSpec((1, 1, d), mod), pl.BlockSpec((1, 1, d), mod),
                  pl.BlockSpec((1, 1, d), mod),
                  pl.BlockSpec(wq_t.shape, const2),
                  pl.BlockSpec(ka.shape, const3), pl.BlockSpec(kb.shape, const3),
                  pl.BlockSpec((ce, d), lambda i, j: (j, 0)),
                  pl.BlockSpec((d, ce), lambda i, j: (0, j)),
                  pl.BlockSpec((1, d), const2), pl.BlockSpec((1, d), const2)],
        out_specs=pl.BlockSpec((tm, d), row),
        out_shape=jax.ShapeDtypeStruct((t, d), F32),
        scratch_shapes=[pltpu.VMEM((d, tm), BF16),
                        pltpu.VMEM((wq_t.shape[0], tm), F32),
                        pltpu.VMEM((PEER_HEADS, PEER_NKEYS, tm), F32),
                        pltpu.VMEM((PEER_HEADS, PEER_NKEYS, tm), F32),
                        pltpu.VMEM((PEER_HEADS, PEER_NKEYS, tm), F32),
                        pltpu.VMEM((ce, tm), F32),
                        pltpu.VMEM((ce, tm), BF16),
                        pltpu.VMEM((d, tm), F32),
                        pltpu.VMEM((3 * SUBLANES, LANES), F32),
                        pltpu.VMEM((3 * SUBLANES, LANES), F32)],
        compiler_params=_cparams(("arbitrary", "arbitrary")),
        name="peer",
    )(h2, sh, sc, gate, wq_t, ka, kb, u_b, vt_b, ln_g.reshape(1, d), ln_b.reshape(1, d))


def _mla_proj_kernel(*refs, rotate, want_q):
    if rotate:
        (h_ref, sh_ref, sc_ref, wd_ref, qg_ref, kvg_ref, wuq_ref, wukv_ref,
         cq_ref, sq_ref, ck_ref, sk_ref) = refs[:12]
        outs = refs[12:]
    else:
        h_ref, sh_ref, sc_ref, wd_ref, qg_ref, kvg_ref, wuq_ref, wukv_ref = refs[:8]
        outs = refs[8:]
    if want_q:
        q_ref, k_ref, v_ref = outs
    else:
        k_ref, v_ref = outs
    nh = MLA_HEADS
    x = h_ref[0] * (1.0 + sc_ref[0]) + sh_ref[0]
    dn = jnp.dot(x.astype(BF16), wd_ref[...], preferred_element_type=F32)
    cq = dn[:, :MLA_Q_RANK]
    ckv = dn[:, MLA_Q_RANK:MLA_Q_RANK + MLA_KV_RANK]
    o = MLA_Q_RANK + MLA_KV_RANK
    kr = dn[:, o:o + MLA_ROPE]
    kr_sw = dn[:, o + MLA_ROPE:o + 2 * MLA_ROPE]
    if rotate:
        kr = kr * ck_ref[...] + kr_sw * sk_ref[...]
    ckvn = ckv * lax.rsqrt(jnp.mean(ckv * ckv, axis=-1, keepdims=True) + RMS_EPS) * kvg_ref[...]
    kv = jnp.dot(ckvn.astype(BF16), wukv_ref[...], preferred_element_type=F32)
    tm = x.shape[0]
    zpad = jnp.zeros((tm, MLA_QK_PAD - MLA_NOPE - MLA_ROPE), F32)
    krp = jnp.concatenate([kr, zpad], axis=-1).astype(BF16)
    for hd in range(nh):
        k_ref[0, hd, :, 0:MLA_NOPE] = kv[:, hd * MLA_NOPE:(hd + 1) * MLA_NOPE].astype(BF16)
        k_ref[0, hd, :, MLA_NOPE:] = krp
        v_ref[0, hd] = kv[:, nh * MLA_NOPE + hd * MLA_V:nh * MLA_NOPE + (hd + 1) * MLA_V].astype(BF16)
    if want_q:
        cqn = cq * lax.rsqrt(jnp.mean(cq * cq, axis=-1, keepdims=True) + RMS_EPS) * qg_ref[...]
        q = jnp.dot(cqn.astype(BF16), wuq_ref[...], preferred_element_type=F32)
        scale = (MLA_NOPE + MLA_ROPE) ** -0.5
        base = nh * MLA_NOPE
        qr = q[:, base:base + nh * MLA_ROPE]
        if rotate:
            qr_sw = q[:, base + nh * MLA_ROPE:]
            qr = qr * cq_ref[...] + qr_sw * sq_ref[...]
        for hd in range(nh):
            q_ref[0, hd, :, 0:MLA_NOPE] = (q[:, hd * MLA_NOPE:(hd + 1) * MLA_NOPE] * scale).astype(BF16)
            qrp = jnp.concatenate([qr[:, hd * MLA_ROPE:(hd + 1) * MLA_ROPE] * scale, zpad], axis=-1)
            q_ref[0, hd, :, MLA_NOPE:] = qrp.astype(BF16)


def _mla_proj(h3, sh, sc, mod_idx, wd, qg, kvg, wuq, wukv, rope, want_q):
    b, n, d = h3.shape
    tm = _tile(n, 256)
    nt = n // tm
    rotate = rope is not None
    tok = lambda i, t: (i, t, 0)
    const = lambda i, t: (0, 0)
    mod = lambda i, t: (mod_idx(i), 0, 0)
    ins = [h3, sh, sc, wd, qg, kvg, wuq, wukv]
    specs = [pl.BlockSpec((1, tm, d), tok), pl.BlockSpec((1, 1, d), mod), pl.BlockSpec((1, 1, d), mod),
             pl.BlockSpec(wd.shape, const), pl.BlockSpec(qg.shape, const), pl.BlockSpec(kvg.shape, const),
             pl.BlockSpec(wuq.shape, const), pl.BlockSpec(wukv.shape, const)]
    if rotate:
        ins += list(rope)
        specs += [pl.BlockSpec((tm, a.shape[1]), lambda i, t: (t, 0)) for a in rope]
    hd4 = lambda i, t: (i, 0, t, 0)
    out_specs, out_shape = [], []
    if want_q:
        out_specs.append(pl.BlockSpec((1, MLA_HEADS, tm, MLA_QK_PAD), hd4))
        out_shape.append(jax.ShapeDtypeStruct((b, MLA_HEADS, n, MLA_QK_PAD), BF16))
    out_specs += [pl.BlockSpec((1, MLA_HEADS, tm, MLA_QK_PAD), hd4),
                  pl.BlockSpec((1, MLA_HEADS, tm, MLA_V), hd4)]
    out_shape += [jax.ShapeDtypeStruct((b, MLA_HEADS, n, MLA_QK_PAD), BF16),
                  jax.ShapeDtypeStruct((b, MLA_HEADS, n, MLA_V), BF16)]

    return pl.pallas_call(
        functools.partial(_mla_proj_kernel, rotate=rotate, want_q=want_q),
        grid=(b, nt),
        in_specs=specs,
        out_specs=out_specs,
        out_shape=out_shape,
        compiler_params=_cparams(("arbitrary", "arbitrary")),
        name="mla_proj_q" if want_q else "mla_proj_kv",
    )(*ins)


def _mla_attn_kernel(q_ref, kc_ref, kl_ref, vc_ref, vl_ref, o_ref):
    q = q_ref[0, 0]
    nt = (((1,), (1,)), ((), ()))
    s_c = lax.dot_general(q, kc_ref[0, 0], nt, preferred_element_type=F32)
    s_l = lax.dot_general(q, kl_ref[0, 0], nt, preferred_element_type=F32)
    m = jnp.maximum(jnp.max(s_c, axis=-1, keepdims=True), jnp.max(s_l, axis=-1, keepdims=True))
    p_c = jnp.exp(s_c - m)
    p_l = jnp.exp(s_l - m)
    den = jnp.sum(p_c, axis=-1, keepdims=True) + jnp.sum(p_l, axis=-1, keepdims=True)
    o = (jnp.dot(p_c.astype(BF16), vc_ref[0, 0], preferred_element_type=F32)
         + jnp.dot(p_l.astype(BF16), vl_ref[0, 0], preferred_element_type=F32))
    o_ref[0] = (o / den).astype(o_ref.dtype)


def _mla_attn(q, kc, kl, vc, vl):
    b, nh, n, dq = q.shape
    lc = kc.shape[2]
    tq = _tile(n, 256)
    kv = lambda i, h, t: (i, h, 0, 0)
    return pl.pallas_call(
        _mla_attn_kernel,
        grid=(b, nh, n // tq),
        in_specs=[pl.BlockSpec((1, 1, tq, dq), lambda i, h, t: (i, h, t, 0)),
                  pl.BlockSpec((1, 1, lc, dq), kv), pl.BlockSpec((1, 1, n, dq), kv),
                  pl.BlockSpec((1, 1, lc, MLA_V), kv), pl.BlockSpec((1, 1, n, MLA_V), kv)],
        out_specs=pl.BlockSpec((1, tq, MLA_V), lambda i, h, t: (i, t, h)),
        out_shape=jax.ShapeDtypeStruct((b, n, nh * MLA_V), BF16),
        compiler_params=_cparams(("arbitrary", "arbitrary", "arbitrary")),
        name="mla_attn",
    )(q, kc, kl, vc, vl)


def _rope_tables(n_tokens, reps):
    rows = n_tokens // GRID_W
    row = jnp.repeat(jnp.arange(rows, dtype=F32), GRID_W)
    col = jnp.tile(jnp.arange(GRID_W, dtype=F32), rows)
    n_freq = MLA_ROPE // 4
    inv_freq = ROPE_BASE ** (-jnp.arange(n_freq, dtype=F32) / n_freq)
    ar = row[:, None] * inv_freq
    ac = col[:, None] * inv_freq
    cos = jnp.concatenate([jnp.cos(ar), jnp.cos(ar), jnp.cos(ac), jnp.cos(ac)], axis=-1)
    sin = jnp.concatenate([-jnp.sin(ar), jnp.sin(ar), -jnp.sin(ac), jnp.sin(ac)], axis=-1)
    return jnp.tile(cos, (1, reps)), jnp.tile(sin, (1, reps))


def _swap_halves_cols(w):
    q = MLA_ROPE // 4
    return jnp.concatenate([w[..., q:2 * q], w[..., 0:q], w[..., 3 * q:4 * q], w[..., 2 * q:3 * q]], axis=-1)


def kernel(x, c, ctx, c_ctx, ada_w, ada_b, ln_tm_g, ln_tm_b, ln_cm_g, ln_cm_b, gla_w_in, gla_gate_fwd_a, gla_gate_fwd_b, gla_gate_fwd_bias, gla_gate_bwd_a, gla_gate_bwd_b, gla_gate_bwd_bias, gla_norm_g, gla_w_out, mla_w_down, mla_q_norm_g, mla_kv_norm_g, mla_w_uq, mla_w_ukv, mla_w_out, peer_w_query, peer_keys_a, peer_keys_b, peer_u, peer_v):
    b, ll, d = x.shape
    lc = ctx.shape[1]
    tl, tc = b * ll, b * lc
    assert DEPTH == ada_w.shape[0] == 2

    nb = -(-(b + 1) // SUBLANES) * SUBLANES
    c_all = jnp.concatenate([c, c_ctx[None, :], jnp.zeros((nb - b - 1, d), F32)], axis=0)
    mod = _ada(c_all, ada_w, ada_b).reshape(DEPTH, nb, 6, 1, d)

    def lat_idx(tokens_per_row):
        return lambda i, tm: (i * tm) // tokens_per_row

    lat_mod = lat_idx(ll)
    ctx_mod = lambda i, tm: b

    h_lat = x.reshape(tl, d)
    h_ctx = ctx.reshape(tc, d)

    def peer_weights(i):
        wq_t = peer_w_query[i].T.astype(BF16)
        return (wq_t, peer_keys_a[i].astype(BF16), peer_keys_b[i].astype(BF16),
                peer_u[i].astype(BF16), peer_v[i].T.astype(BF16))

    m0 = [mod[0, :, k] for k in range(6)]
    hk = gla_gate_fwd_b.shape[2]
    hv = (gla_w_in.shape[2] - 2 * hk) // 2
    w_in = gla_w_in[0].astype(BF16)
    ga = jnp.concatenate([gla_gate_fwd_a[0], gla_gate_bwd_a[0]], axis=1).astype(BF16)
    zr = jnp.zeros((GLA_GATE_RANK, hk), F32)
    gb = jnp.concatenate([jnp.concatenate([gla_gate_fwd_b[0], zr], axis=1),
                          jnp.concatenate([zr, gla_gate_bwd_b[0]], axis=1)], axis=0).astype(BF16)
    gbias = jnp.concatenate([gla_gate_fwd_bias[0], gla_gate_bwd_bias[0]])[None, :]
    proj_l = _gla_inproj(h_lat, m0[0], m0[1], lat_mod, w_in, ga, gb, gbias, hk, hv)
    proj_c = _gla_inproj(h_ctx, m0[0], m0[1], ctx_mod, w_in, ga, gb, gbias, hk, hv)
    on_c, on_l = _gla_scan(proj_c, proj_l, gla_norm_g[0], b, lc, ll, hk, hv)
    w_out = gla_w_out[0].astype(BF16)
    h_lat = _mix_out(on_l.reshape(tl, hv), proj_l[3], h_lat, m0[2], lat_mod, w_out, ln_tm_g[0], ln_tm_b[0])
    h_ctx = _mix_out(on_c.reshape(tc, hv), proj_c[3], h_ctx, m0[2], ctx_mod, w_out, ln_tm_g[0], ln_tm_b[0])
    pw = peer_weights(0)
    h_lat = _peer(h_lat, m0[3], m0[4], m0[5], lat_mod, *pw, ln_cm_g[0], ln_cm_b[0])
    h_ctx = _peer(h_ctx, m0[3], m0[4], m0[5], ctx_mod, *pw, ln_cm_g[0], ln_cm_b[0])

    m1 = [mod[1, :, k] for k in range(6)]
    nh = MLA_HEADS
    wd = mla_w_down[0]
    o = MLA_Q_RANK + MLA_KV_RANK
    wd = jnp.concatenate([wd, _swap_halves_cols(wd[:, o:])], axis=1).astype(BF16)
    wuq = mla_w_uq[0].reshape(MLA_Q_RANK, nh, MLA_NOPE + MLA_ROPE)
    wuq_rope = wuq[:, :, MLA_NOPE:]
    wuq = jnp.concatenate([wuq[:, :, :MLA_NOPE].reshape(MLA_Q_RANK, -1),
                           wuq_rope.reshape(MLA_Q_RANK, -1),
                           _swap_halves_cols(wuq_rope).reshape(MLA_Q_RANK, -1)], axis=1).astype(BF16)
    wukv = mla_w_ukv[0].reshape(MLA_KV_RANK, nh, MLA_NOPE + MLA_V)
    wukv = jnp.concatenate([wukv[:, :, :MLA_NOPE].reshape(MLA_KV_RANK, -1),
                            wukv[:, :, MLA_NOPE:].reshape(MLA_KV_RANK, -1)], axis=1).astype(BF16)
    qg = mla_q_norm_g[0][None, :]
    kvg = mla_kv_norm_g[0][None, :]
    cos_q, sin_q = _rope_tables(ll, nh)
    cos_k, sin_k = _rope_tables(ll, 1)
    q_l, k_l, v_l = _mla_proj(h_lat.reshape(b, ll, d), m1[0], m1[1], lambda i: i, wd, qg, kvg, wuq, wukv,
                              (cos_q, sin_q, cos_k, sin_k), True)
    k_c, v_c = _mla_proj(h_ctx.reshape(b, lc, d), m1[0], m1[1], lambda i: b, wd, qg, kvg, wuq, wukv,
                         None, False)
    o_l = _mla_attn(q_l, k_c, k_l, v_c, v_l)
    h_lat = _mix_out(o_l.reshape(tl, nh * MLA_V), None, h_lat, m1[2], lat_mod, mla_w_out[0].astype(BF16),
                     ln_tm_g[1], ln_tm_b[1])
    pw = peer_weights(1)
    h_lat = _peer(h_lat, m1[3], m1[4], m1[5], lat_mod, *pw, ln_cm_g[1], ln_cm_b[1])
    return h_lat.reshape(b, ll, d)
```

```python
import functools
import math

import jax
import jax.numpy as jnp
from jax import lax
from jax.experimental import pallas as pl
from jax.experimental.pallas import tpu as pltpu

F32 = jnp.float32
BF16 = jnp.bfloat16
NEG_INF = float("-inf")

DEPTH = 2
ALPHA = (2.0 * DEPTH) ** 0.25
LN_EPS = 1e-5
RMS_EPS = 1e-6

GLA_HEADS = 4
GLA_TAU = 16.0
GLA_CHUNK = 64
GLA_GATE_RANK = 16

MLA_HEADS = 8
MLA_NOPE = 128
MLA_ROPE = 64
MLA_V = 128
MLA_Q_RANK = 256
MLA_KV_RANK = 128
MLA_QK_PAD = 256
GRID_W = 64
ROPE_BASE = 10000.0

PEER_HEADS = 8
PEER_NKEYS = 128
PEER_TOPK = 16
PEER_NEXT = PEER_TOPK + 1

LANES = 128
SUBLANES = 8
VMEM_LIMIT = 56 * 1024 * 1024


def _cparams(sem, flags=None):
    return pltpu.CompilerParams(dimension_semantics=sem, vmem_limit_bytes=VMEM_LIMIT, flags=flags)


def _layer_norm(z, g, b):
    mu = jnp.mean(z, axis=-1, keepdims=True)
    zc = z - mu
    var = jnp.mean(zc * zc, axis=-1, keepdims=True)
    return zc * lax.rsqrt(var + LN_EPS) * g + b


def _silu(x):
    return x / (1.0 + jnp.exp(-x))


def _tile(n, pref):
    t = pref
    while n % t:
        t //= 2
    return t


def _ada_kernel(c_ref, w_ref, b_ref, o_ref):
    s = _silu(c_ref[...])
    o_ref[0] = jnp.dot(s, w_ref[0], preferred_element_type=F32,
                       precision=lax.Precision.HIGHEST) + b_ref[0]


def _ada(c_all, ada_w, ada_b):
    nb, d = c_all.shape
    depth, _, n6 = ada_w.shape
    tn = 1024
    return pl.pallas_call(
        _ada_kernel,
        grid=(depth, n6 // tn),
        in_specs=[pl.BlockSpec((nb, d), lambda i, j: (0, 0)),
                  pl.BlockSpec((1, d, tn), lambda i, j: (i, 0, j)),
                  pl.BlockSpec((1, 1, tn), lambda i, j: (i, 0, j))],
        out_specs=pl.BlockSpec((1, nb, tn), lambda i, j: (i, 0, j)),
        out_shape=jax.ShapeDtypeStruct((depth, nb, n6), F32),
        compiler_params=_cparams(("arbitrary", "arbitrary")),
        name="ada_mod",
    )(c_all, ada_w, ada_b.reshape(depth, 1, n6))


def _gla_inproj_kernel(h_ref, sh_ref, sc_ref, w_ref, ga_ref, gb_ref, gbias_ref,
                       q_ref, k_ref, v_ref, r_ref, lf_ref, lb_ref, *, hk, hv, dk):
    x = h_ref[...] * (1.0 + sc_ref[0]) + sh_ref[0]
    xb = x.astype(BF16)
    y = jnp.dot(xb, w_ref[...], preferred_element_type=F32)
    q_ref[...] = y[:, :hk] * (dk ** -0.5)
    k_ref[...] = y[:, hk:2 * hk]
    v_ref[...] = y[:, 2 * hk:2 * hk + hv]
    r_ref[...] = y[:, 2 * hk + hv:]
    g1 = jnp.dot(xb, ga_ref[...], preferred_element_type=F32)
    z = jnp.dot(g1.astype(BF16), gb_ref[...], preferred_element_type=F32) + gbias_ref[...]
    ls = (jnp.minimum(z, 0.0) - jnp.log1p(jnp.exp(-jnp.abs(z)))) * (1.0 / GLA_TAU)
    lf_ref[...] = ls[:, :hk]
    lb_ref[...] = ls[:, hk:]


def _gla_inproj(h2, sh, sc, mod_idx, w_in, ga, gb, gbias, hk, hv):
    t, d = h2.shape
    tm = _tile(t, 256)
    row = lambda i: (i, 0)
    const = lambda i: (0, 0)
    kern = functools.partial(_gla_inproj_kernel, hk=hk, hv=hv, dk=hk // GLA_HEADS)
    return pl.pallas_call(
        kern,
        grid=(t // tm,),
        in_specs=[pl.BlockSpec((tm, d), row),
                  pl.BlockSpec((1, 1, d), lambda i: (mod_idx(i, tm), 0, 0)),
                  pl.BlockSpec((1, 1, d), lambda i: (mod_idx(i, tm), 0, 0)),
                  pl.BlockSpec(w_in.shape, const),
                  pl.BlockSpec(ga.shape, const),
                  pl.BlockSpec(gb.shape, const),
                  pl.BlockSpec(gbias.shape, const)],
        out_specs=[pl.BlockSpec((tm, hk), row), pl.BlockSpec((tm, hk), row),
                   pl.BlockSpec((tm, hv), row), pl.BlockSpec((tm, hv), row),
                   pl.BlockSpec((tm, hk), row), pl.BlockSpec((tm, hk), row)],
        out_shape=[jax.ShapeDtypeStruct((t, hk), F32), jax.ShapeDtypeStruct((t, hk), F32),
                   jax.ShapeDtypeStruct((t, hv), F32), jax.ShapeDtypeStruct((t, hv), F32),
                   jax.ShapeDtypeStruct((t, hk), F32), jax.ShapeDtypeStruct((t, hk), F32)],
        compiler_params=_cparams(("arbitrary",)),
        name="gla_inproj",
    )(h2, sh, sc, w_in, ga, gb, gbias)


def _split3(x):
    x1 = x.astype(BF16)
    r1 = x - x1.astype(F32)
    x2 = r1.astype(BF16)
    x3 = (r1 - x2.astype(F32)).astype(BF16)
    return x1, x2, x3


def _gla_chunk(q, k, v, lg, st_ref, tri, mask, last_row):
    l1, l2, l3 = _split3(lg)
    cum = (jnp.dot(tri, l1, preferred_element_type=F32)
           + jnp.dot(tri, l2, preferred_element_type=F32)
           + jnp.dot(tri, l3, preferred_element_type=F32))
    last = cum[last_row:last_row + 1, :]
    qe = (q * jnp.exp(cum)).astype(BF16)
    ke = (k * jnp.exp(-cum)).astype(BF16)
    kd = (k * jnp.exp(last - cum)).astype(BF16)
    vb = v.astype(BF16)
    a = lax.dot_general(qe, ke, (((1,), (1,)), ((), ())), preferred_element_type=F32)
    a = jnp.where(mask, a, 0.0)
    st = st_ref[...]
    o = jnp.dot(a.astype(BF16), vb, preferred_element_type=F32)
    o = o + lax.dot_general(qe, st.astype(BF16), (((1,), (1,)), ((), ())),
                            preferred_element_type=F32)
    kvt = lax.dot_general(vb, kd, (((0,), (0,)), ((), ())), preferred_element_type=F32)
    st_ref[...] = st * jnp.exp(last) + kvt
    return o


def _gla_scan_kernel(qc_ref, kc_ref, vc_ref, lfc_ref, lbc_ref,
                     ql_ref, kl_ref, vl_ref, lfl_ref, lbl_ref, ng_ref,
                     oc_ref, ol_ref, stf_ref, stb_ref):
    c = GLA_CHUNK
    ii = lax.broadcasted_iota(jnp.int32, (c, c), 0)
    jj = lax.broadcasted_iota(jnp.int32, (c, c), 1)
    mask_f = ii >= jj
    mask_b = ii < jj
    tri_f = jnp.where(mask_f, 1.0, 0.0).astype(BF16)
    tri_b = jnp.where(ii <= jj, 1.0, 0.0).astype(BF16)

    stf_ref[...] = jnp.zeros_like(stf_ref)
    stb_ref[...] = jnp.zeros_like(stb_ref)
    oc_ref[...] = jnp.zeros_like(oc_ref)
    ol_ref[...] = jnp.zeros_like(ol_ref)

    def run(q_ref, k_ref, v_ref, lf_ref, lb_ref, o_ref):
        n = q_ref.shape[1] // c

        def body(i, carry):
            sf = pl.ds(pl.multiple_of(i * c, c), c)
            sb = pl.ds(pl.multiple_of((n - 1 - i) * c, c), c)
            of = _gla_chunk(q_ref[0, sf, :], k_ref[0, sf, :], v_ref[0, sf, :], lf_ref[0, sf, :],
                            stf_ref, tri_f, mask_f, c - 1)
            o_ref[0, sf, :] += of
            ob = _gla_chunk(q_ref[0, sb, :], k_ref[0, sb, :], v_ref[0, sb, :], lb_ref[0, sb, :],
                            stb_ref, tri_b, mask_b, 0)
            o_ref[0, sb, :] += ob
            return carry

        lax.fori_loop(0, n, body, 0)

    run(qc_ref, kc_ref, vc_ref, lfc_ref, lbc_ref, oc_ref)
    run(ql_ref, kl_ref, vl_ref, lfl_ref, lbl_ref, ol_ref)

    g = ng_ref[...]

    def norm(o_ref):
        rows = o_ref.shape[1]
        blk = _tile(rows, 256)

        def body(i, carry):
            s = pl.ds(pl.multiple_of(i * blk, blk), blk)
            o = o_ref[0, s, :]
            o_ref[0, s, :] = o * lax.rsqrt(jnp.mean(o * o, axis=-1, keepdims=True) + RMS_EPS) * g
            return carry

        lax.fori_loop(0, rows // blk, body, 0)

    norm(oc_ref)
    norm(ol_ref)


def _gla_scan(proj_c, proj_l, norm_g, b, lc, ll, hk, hv):
    dk, dv = hk // GLA_HEADS, hv // GLA_HEADS
    qc, kc, vc, _, lfc, lbc = [a.reshape(b, lc, -1) for a in proj_c]
    ql, kl, vl, _, lfl, lbl = [a.reshape(b, ll, -1) for a in proj_l]
    head = lambda i, h: (i, 0, h)
    specs = []
    for n in (lc, ll):
        specs += [pl.BlockSpec((1, n, dk), head), pl.BlockSpec((1, n, dk), head),
                  pl.BlockSpec((1, n, dv), head), pl.BlockSpec((1, n, dk), head),
                  pl.BlockSpec((1, n, dk), head)]
    specs.append(pl.BlockSpec((1, dv), lambda i, h: (0, 0)))
    oc, ol = pl.pallas_call(
        _gla_scan_kernel,
        grid=(b, GLA_HEADS),
        in_specs=specs,
        out_specs=[pl.BlockSpec((1, lc, dv), head), pl.BlockSpec((1, ll, dv), head)],
        out_shape=[jax.ShapeDtypeStruct((b, lc, hv), F32), jax.ShapeDtypeStruct((b, ll, hv), F32)],
        scratch_shapes=[pltpu.VMEM((dv, dk), F32), pltpu.VMEM((dv, dk), F32)],
        compiler_params=_cparams(("arbitrary", "arbitrary")),
        name="gla_scan",
    )(qc, kc, vc, lfc, lbc, ql, kl, vl, lfl, lbl, norm_g.reshape(1, dv))
    return oc, ol


def _mix_out_kernel(*refs, gated):
    if gated:
        o_ref, r_ref, h_ref, g_ref, w_ref, lng_ref, lnb_ref, out_ref = refs
        o = o_ref[...] * _silu(r_ref[...])
    else:
        o_ref, h_ref, g_ref, w_ref, lng_ref, lnb_ref, out_ref = refs
        o = o_ref[...]
    y = jnp.dot(o.astype(BF16), w_ref[...], preferred_element_type=F32)
    z = ALPHA * h_ref[...] + g_ref[0] * y
    out_ref[...] = _layer_norm(z, lng_ref[...], lnb_ref[...])


def _mix_out(o2, r2, h2, gate, mod_idx, w_out, ln_g, ln_b):
    t, d = h2.shape
    kdim = o2.shape[1]
    tm = _tile(t, 512)
    row = lambda i: (i, 0)
    const = lambda i: (0, 0)
    gated = r2 is not None
    ins = [o2] + ([r2] if gated else []) + [h2, gate, w_out, ln_g.reshape(1, d), ln_b.reshape(1, d)]
    specs = [pl.BlockSpec((tm, kdim), row)] + ([pl.BlockSpec((tm, kdim), row)] if gated else [])
    specs += [pl.BlockSpec((tm, d), row),
              pl.BlockSpec((1, 1, d), lambda i: (mod_idx(i, tm), 0, 0)),
              pl.BlockSpec(w_out.shape, const),
              pl.BlockSpec((1, d), const), pl.BlockSpec((1, d), const)]
    return pl.pallas_call(
        functools.partial(_mix_out_kernel, gated=gated),
        grid=(t // tm,),
        in_specs=specs,
        out_specs=pl.BlockSpec((tm, d), row),
        out_shape=jax.ShapeDtypeStruct((t, d), F32),
        compiler_params=_cparams(("arbitrary",)),
        name="mix_out",
    )(*ins)


def _oddeven_merge_sort_pairs(n):
    pairs = []
    p = 1
    while p < n:
        k = p
        while k >= 1:
            for j in range(k % p, n - k, 2 * k):
                for i in range(min(k, n - j - k)):
                    if (i + j) // (2 * p) == (i + j + k) // (2 * p):
                        pairs.append((i + j, i + j + k))
            k //= 2
        p *= 2
    return pairs


def _sublane_allmax(x):
    return jnp.broadcast_to(jnp.max(x, axis=0, keepdims=True), x.shape)


def _merge_top(rows, n):
    rows = list(rows)
    out = []
    for k in range(n):
        head = rows[0]
        m = _sublane_allmax(head)
        out.append(m)
        rem = n - 1 - k
        if rem == 0:
            break
        hit = head >= m
        for r in range(min(rem, len(rows))):
            nxt = rows[r + 1] if r + 1 < len(rows) else NEG_INF
            rows[r] = jnp.where(hit, nxt, rows[r])
    return out


def _peer_route(sa, sb):
    n = PEER_NEXT
    nr = PEER_NKEYS // SUBLANES

    def top(s):
        rows = [s[r * SUBLANES:(r + 1) * SUBLANES, :] for r in range(nr)]
        for i, j in _oddeven_merge_sort_pairs(nr):
            rows[i], rows[j] = jnp.maximum(rows[i], rows[j]), jnp.minimum(rows[i], rows[j])
        return _merge_top(rows, n)

    va = top(sa)
    vb = top(sb)
    sub = lax.broadcasted_iota(jnp.int32, (SUBLANES, LANES), 0)
    nj = 4
    lens = [n // (j + 1) for j in range(nj)]
    lens += [sum(1 for j in range(nj, n) if (i + 1) * (j + 1) <= n) for i in range(SUBLANES - nj)]
    assert sum(lens) == sum(n // (j + 1) for j in range(n)) and lens[-1] == 0
    len_t = jnp.zeros((SUBLANES, LANES), jnp.int32)
    vb_lo = vb[0]
    va_hi = va[0]
    for s in range(SUBLANES):
        len_t = jnp.where(sub == s, lens[s], len_t)
        if 0 < s < nj:
            vb_lo = jnp.where(sub == s, vb[s], vb_lo)
        if s > nj:
            va_hi = jnp.where(sub == s, va[s - nj], va_hi)
    cand = []
    for r in range(max(lens)):
        row = jnp.where(sub < nj, va[r] + vb_lo, va_hi + vb[min(nj + r, n - 1)])
        cand.append(jnp.where(len_t > r, row, NEG_INF))
    e = _merge_top(cand, n)
    tau = 0.5 * (e[PEER_TOPK - 1] + e[PEER_TOPK])
    z = jnp.ones_like(e[0])
    for k in range(1, PEER_TOPK):
        z = z + jnp.exp(e[k] - e[0])
    zinv = 1.0 / z
    tb = tau - vb[0]
    ea, eb, th = [], [], []
    for r in range(nr):
        sar = sa[r * SUBLANES:(r + 1) * SUBLANES, :]
        sbr = sb[r * SUBLANES:(r + 1) * SUBLANES, :]
        ea.append(jnp.exp(sar - va[0]) * zinv)
        eb.append(jnp.exp(sbr - vb[0]))
        th.append(jnp.exp(tb - sar))
    return (jnp.concatenate(ea, axis=0), jnp.concatenate(eb, axis=0), jnp.concatenate(th, axis=0))


def _peer_build_rows(p_ref, ht_ref, ea_ref, eb_ref, th_ref, arow0, ar, al, live):
    nk = PEER_NKEYS
    nlt = p_ref.shape[1] // LANES
    arows = pl.ds(pl.multiple_of(arow0, SUBLANES), SUBLANES)
    r = slice(al * nk, (al + 1) * nk)
    for lt in range(nlt):
        ls = slice(lt * LANES, (lt + 1) * LANES)
        w = jnp.zeros((nk, LANES), F32)
        for hd in range(PEER_HEADS):
            thr = th_ref[hd, arows, ls][ar:ar + 1, :]
            ear = ea_ref[hd, arows, ls][ar:ar + 1, :]
            if live is not None:
                ear = jnp.where(live, ear, 0.0)
            ebv = eb_ref[hd, :, ls]
            w = w + jnp.where(ebv >= thr, ebv, 0.0) * ear
        hh = ht_ref[r, ls]
        gl = 0.5 * hh * (1.0 + lax.erf(hh * (2.0 ** -0.5)))
        p_ref[r, ls] = (w * gl).astype(BF16)


def _peer_kernel(hp_ref, sh_ref, sc_ref, he_ref, gate_ref, wq_ref, ka_ref, kb_ref, u_ref, vt_ref,
                 lng_ref, lnb_ref, out_ref,
                 xt_s, q_s, ea_s, eb_s, th_s, ht0_s, ht1_s, p0_s, p1_s, yt_s, *, n_tiles):
    g = pl.program_id(0)
    tm = hp_ref.shape[0]
    nlt = tm // LANES
    nk = PEER_NKEYS
    half = u_ref.shape[0] // 2
    a_half = half // nk
    pairs = nk // (2 * a_half)
    tile = g // pairs
    jp = g % pairs
    par = tile % 2
    gm1 = jnp.maximum(g - 1, 0)
    tile_prev = gm1 // pairs
    par_prev = tile_prev % 2
    jp_prev = gm1 % pairs

    @pl.when(g == 0)
    def _init():
        for ref in (ht0_s, ht1_s, p0_s, p1_s, yt_s, ea_s, eb_s, th_s):
            ref[...] = jnp.zeros_like(ref)

    @pl.when((jp == 0) & (tile < n_tiles))
    def _prologue():
        x = hp_ref[...] * (1.0 + sc_ref[0]) + sh_ref[0]
        xt_s[par] = x.T.astype(BF16)

        def head_body(hd, carry):
            rq = pl.ds(pl.multiple_of(hd * 2 * nk, 2 * nk), 2 * nk)
            q_s[...] = jnp.dot(wq_ref[rq, :], xt_s[par], preferred_element_type=F32)
            sa = jnp.dot(ka_ref[hd], q_s[0:nk, :].astype(BF16), preferred_element_type=F32)
            sb = jnp.dot(kb_ref[hd], q_s[nk:, :].astype(BF16), preferred_element_type=F32)
            for lt in range(nlt):
                ls = slice(lt * LANES, (lt + 1) * LANES)
                ea, eb, th = _peer_route(sa[:, ls], sb[:, ls])
                ea_s[par, hd, :, ls] = ea
                eb_s[par, hd, :, ls] = eb
                th_s[par, hd, :, ls] = th
            return carry

        lax.fori_loop(0, PEER_HEADS, head_body, 0)

    xt = xt_s.at[par]
    prev = (ea_s.at[par_prev], eb_s.at[par_prev], th_s.at[par_prev], jp_prev * 2 * a_half)
    cur = (ea_s.at[par], eb_s.at[par], th_s.at[par], jp * 2 * a_half)
    dq = yt_s.shape[0] // a_half
    for k in range(a_half):
        rk = slice(k * nk, (k + 1) * nk)
        rd = slice(k * dq, (k + 1) * dq)
        ht0_s[rk, :] = jnp.dot(u_ref[rk, :], xt[...], preferred_element_type=F32)
        yt_s[rd, :] += jnp.dot(vt_ref[rd, 0:half], p0_s[...], preferred_element_type=F32)
        _peer_build_rows(p1_s, ht1_s, *prev, a_half + k, k, g > 0)
    @pl.when(jp >= 0)
    def _half_b():
        for k in range(a_half):
            rk = slice(k * nk, (k + 1) * nk)
            rd = slice(k * dq, (k + 1) * dq)
            ht1_s[rk, :] = jnp.dot(u_ref[half + k * nk:half + (k + 1) * nk, :], xt[...],
                                   preferred_element_type=F32)
            yt_s[rd, :] += jnp.dot(vt_ref[rd, half:], p1_s[...], preferred_element_type=F32)
            _peer_build_rows(p0_s, ht0_s, *cur, k, k, None)

    @pl.when((jp == 0) & (g > 0))
    def _epilogue():
        y = yt_s[...].T
        z = ALPHA * he_ref[...] + gate_ref[0] * y
        out_ref[...] = _layer_norm(z, lng_ref[...], lnb_ref[...])
        yt_s[...] = jnp.zeros_like(yt_s)


def _peer(h2, sh, sc, gate, mod_idx, wq_t, ka, kb, u_b, vt_b, ln_g, ln_b):
    t, d = h2.shape
    ne = u_b.shape[0]
    tm = _tile(t, 512)
    n_tiles = t // tm
    ce = SUBLANES * PEER_NKEYS
    half = ce // 2
    pairs = ne // ce
    const2 = lambda g: (0, 0)
    const3 = lambda g: (0, 0, 0)
    pro_tile = lambda g: jnp.minimum(g // pairs, n_tiles - 1)
    epi_tile = lambda g: jnp.maximum(g - 1, 0) // pairs
    factors = (2, PEER_HEADS, PEER_NKEYS, tm)
    return pl.pallas_call(
        functools.partial(_peer_kernel, n_tiles=n_tiles),
        grid=(n_tiles * pairs + 1,),
        in_specs=[pl.BlockSpec((tm, d), lambda g: (pro_tile(g), 0)),
                  pl.BlockSpec((1, 1, d), lambda g: (mod_idx(pro_tile(g), tm), 0, 0)),
                  pl.BlockSpec((1, 1, d), lambda g: (mod_idx(pro_tile(g), tm), 0, 0)),
                  pl.BlockSpec((tm, d), lambda g: (epi_tile(g), 0)),
                  pl.BlockSpec((1, 1, d), lambda g: (mod_idx(epi_tile(g), tm), 0, 0)),
                  pl.BlockSpec(wq_t.shape, const2),
                  pl.BlockSpec(ka.shape, const3),
                  pl.BlockSpec(kb.shape, const3),
                  pl.BlockSpec((ce, d), lambda g: (g % pairs, 0)),
                  pl.BlockSpec((d, ce), lambda g: (0, (g + pairs - 1) % pairs)),
                  pl.BlockSpec((1, d), const2), pl.BlockSpec((1, d), const2)],
        out_specs=pl.BlockSpec((tm, d), lambda g: (epi_tile(g), 0)),
        out_shape=jax.ShapeDtypeStruct((t, d), F32),
        scratch_shapes=[pltpu.VMEM((2, d, tm), BF16),
                        pltpu.VMEM((2 * PEER_NKEYS, tm), F32),
                        pltpu.VMEM(factors, F32), pltpu.VMEM(factors, F32), pltpu.VMEM(factors, F32),
                        pltpu.VMEM((half, tm), F32), pltpu.VMEM((half, tm), F32),
                        pltpu.VMEM((half, tm), BF16), pltpu.VMEM((half, tm), BF16),
                        pltpu.VMEM((d, tm), F32)],
        compiler_params=_cparams(("arbitrary",), None),
        name="peer",
    )(h2, sh, sc, h2, gate, wq_t, ka, kb, u_b, vt_b, ln_g.reshape(1, d), ln_b.reshape(1, d))


def _mla_proj_kernel(*refs, rotate, want_q):
    if rotate:
        (h_ref, sh_ref, sc_ref, wd_ref, qg_ref, kvg_ref, wuq_ref, wukv_ref,
         cq_ref, sq_ref, ck_ref, sk_ref) = refs[:12]
        outs = refs[12:]
    else:
        h_ref, sh_ref, sc_ref, wd_ref, qg_ref, kvg_ref, wuq_ref, wukv_ref = refs[:8]
        outs = refs[8:]
    if want_q:
        q_ref, k_ref, v_ref = outs
    else:
        k_ref, v_ref = outs
    nh = MLA_HEADS
    x = h_ref[0] * (1.0 + sc_ref[0]) + sh_ref[0]
    dn = jnp.dot(x.astype(BF16), wd_ref[...], preferred_element_type=F32)
    cq = dn[:, :MLA_Q_RANK]
    ckv = dn[:, MLA_Q_RANK:MLA_Q_RANK + MLA_KV_RANK]
    o = MLA_Q_RANK + MLA_KV_RANK
    kr = dn[:, o:o + MLA_ROPE]
    kr_sw = dn[:, o + MLA_ROPE:o + 2 * MLA_ROPE]
    if rotate:
        kr = kr * ck_ref[...] + kr_sw * sk_ref[...]
    ckvn = ckv * lax.rsqrt(jnp.mean(ckv * ckv, axis=-1, keepdims=True) + RMS_EPS) * kvg_ref[...]
    kv = jnp.dot(ckvn.astype(BF16), wukv_ref[...], preferred_element_type=F32)
    tm = x.shape[0]
    zpad = jnp.zeros((tm, MLA_QK_PAD - MLA_NOPE - MLA_ROPE), F32)
    krp = jnp.concatenate([kr, zpad], axis=-1).astype(BF16)
    for hd in range(nh):
        k_ref[0, hd, :, 0:MLA_NOPE] = kv[:, hd * MLA_NOPE:(hd + 1) * MLA_NOPE].astype(BF16)
        k_ref[0, hd, :, MLA_NOPE:] = krp
        v_ref[0, hd] = kv[:, nh * MLA_NOPE + hd * MLA_V:nh * MLA_NOPE + (hd + 1) * MLA_V].astype(BF16)
    if want_q:
        cqn = cq * lax.rsqrt(jnp.mean(cq * cq, axis=-1, keepdims=True) + RMS_EPS) * qg_ref[...]
        q = jnp.dot(cqn.astype(BF16), wuq_ref[...], preferred_element_type=F32)
        scale = (MLA_NOPE + MLA_ROPE) ** -0.5
        base = nh * MLA_NOPE
        qr = q[:, base:base + nh * MLA_ROPE]
        if rotate:
            qr_sw = q[:, base + nh * MLA_ROPE:]
            qr = qr * cq_ref[...] + qr_sw * sq_ref[...]
        for hd in range(nh):
            q_ref[0, hd, :, 0:MLA_NOPE] = (q[:, hd * MLA_NOPE:(hd + 1) * MLA_NOPE] * scale).astype(BF16)
            qrp = jnp.concatenate([qr[:, hd * MLA_ROPE:(hd + 1) * MLA_ROPE] * scale, zpad], axis=-1)
            q_ref[0, hd, :, MLA_NOPE:] = qrp.astype(BF16)


def _mla_proj(h3, sh, sc, mod_idx, wd, qg, kvg, wuq, wukv, rope, want_q):
    b, n, d = h3.shape
    tm = _tile(n, 256)
    nt = n // tm
    rotate = rope is not None
    tok = lambda i, t: (i, t, 0)
    const = lambda i, t: (0, 0)
    mod = lambda i, t: (mod_idx(i), 0, 0)
    ins = [h3, sh, sc, wd, qg, kvg, wuq, wukv]
    specs = [pl.BlockSpec((1, tm, d), tok), pl.BlockSpec((1, 1, d), mod), pl.BlockSpec((1, 1, d), mod),
             pl.BlockSpec(wd.shape, const), pl.BlockSpec(qg.shape, const), pl.BlockSpec(kvg.shape, const),
             pl.BlockSpec(wuq.shape, const), pl.BlockSpec(wukv.shape, const)]
    if rotate:
        ins += list(rope)
        specs += [pl.BlockSpec((tm, a.shape[1]), lambda i, t: (t, 0)) for a in rope]
    hd4 = lambda i, t: (i, 0, t, 0)
    out_specs, out_shape = [], []
    if want_q:
        out_specs.append(pl.BlockSpec((1, MLA_HEADS, tm, MLA_QK_PAD), hd4))
        out_shape.append(jax.ShapeDtypeStruct((b, MLA_HEADS, n, MLA_QK_PAD), BF16))
    out_specs += [pl.BlockSpec((1, MLA_HEADS, tm, MLA_QK_PAD), hd4),
                  pl.BlockSpec((1, MLA_HEADS, tm, MLA_V), hd4)]
    out_shape += [jax.ShapeDtypeStruct((b, MLA_HEADS, n, MLA_QK_PAD), BF16),
                  jax.ShapeDtypeStruct((b, MLA_HEADS, n, MLA_V), BF16)]

    return pl.pallas_call(
        functools.partial(_mla_proj_kernel, rotate=rotate, want_q=want_q),
        grid=(b, nt),
        in_specs=specs,
        out_specs=out_specs,
        out_shape=out_shape,
        compiler_params=_cparams(("arbitrary", "arbitrary")),
        name="mla_proj_q" if want_q else "mla_proj_kv",
    )(*ins)


def _mla_attn_kernel(q_ref, kc_ref, kl_ref, vc_ref, vl_ref, o_ref):
    q = q_ref[0, 0]
    nt = (((1,), (1,)), ((), ()))
    s_c = lax.dot_general(q, kc_ref[0, 0], nt, preferred_element_type=F32)
    s_l = lax.dot_general(q, kl_ref[0, 0], nt, preferred_element_type=F32)
    m = jnp.maximum(jnp.max(s_c, axis=-1, keepdims=True), jnp.max(s_l, axis=-1, keepdims=True))
    p_c = jnp.exp(s_c - m)
    p_l = jnp.exp(s_l - m)
    den = jnp.sum(p_c, axis=-1, keepdims=True) + jnp.sum(p_l, axis=-1, keepdims=True)
    o = (jnp.dot(p_c.astype(BF16), vc_ref[0, 0], preferred_element_type=F32)
         + jnp.dot(p_l.astype(BF16), vl_ref[0, 0], preferred_element_type=F32))
    o_ref[0] = (o / den).astype(o_ref.dtype)


def _mla_attn(q, kc, kl, vc, vl):
    b, nh, n, dq = q.shape
    lc = kc.shape[2]
    tq = _tile(n, 512)
    kv = lambda i, h, t: (i, h, 0, 0)
    return pl.pallas_call(
        _mla_attn_kernel,
        grid=(b, nh, n // tq),
        in_specs=[pl.BlockSpec((1, 1, tq, dq), lambda i, h, t: (i, h, t, 0)),
                  pl.BlockSpec((1, 1, lc, dq), kv), pl.BlockSpec((1, 1, n, dq), kv),
                  pl.BlockSpec((1, 1, lc, MLA_V), kv), pl.BlockSpec((1, 1, n, MLA_V), kv)],
        out_specs=pl.BlockSpec((1, tq, MLA_V), lambda i, h, t: (i, t, h)),
        out_shape=jax.ShapeDtypeStruct((b, n, nh * MLA_V), BF16),
        compiler_params=_cparams(("arbitrary", "arbitrary", "arbitrary")),
        name="mla_attn",
    )(q, kc, kl, vc, vl)


def _rope_tables(n_tokens, reps):
    rows = n_tokens // GRID_W
    row = jnp.repeat(jnp.arange(rows, dtype=F32), GRID_W)
    col = jnp.tile(jnp.arange(GRID_W, dtype=F32), rows)
    n_freq = MLA_ROPE // 4
    inv_freq = ROPE_BASE ** (-jnp.arange(n_freq, dtype=F32) / n_freq)
    ar = row[:, None] * inv_freq
    ac = col[:, None] * inv_freq
    cos = jnp.concatenate([jnp.cos(ar), jnp.cos(ar), jnp.cos(ac), jnp.cos(ac)], axis=-1)
    sin = jnp.concatenate([-jnp.sin(ar), jnp.sin(ar), -jnp.sin(ac), jnp.sin(ac)], axis=-1)
    return jnp.tile(cos, (1, reps)), jnp.tile(sin, (1, reps))


def _swap_halves_cols(w):
    q = MLA_ROPE // 4
    return jnp.concatenate([w[..., q:2 * q], w[..., 0:q], w[..., 3 * q:4 * q], w[..., 2 * q:3 * q]], axis=-1)


def kernel(x, c, ctx, c_ctx, ada_w, ada_b, ln_tm_g, ln_tm_b, ln_cm_g, ln_cm_b, gla_w_in, gla_gate_fwd_a, gla_gate_fwd_b, gla_gate_fwd_bias, gla_gate_bwd_a, gla_gate_bwd_b, gla_gate_bwd_bias, gla_norm_g, gla_w_out, mla_w_down, mla_q_norm_g, mla_kv_norm_g, mla_w_uq, mla_w_ukv, mla_w_out, peer_w_query, peer_keys_a, peer_keys_b, peer_u, peer_v):
    b, ll, d = x.shape
    lc = ctx.shape[1]
    tl, tc = b * ll, b * lc
    assert DEPTH == ada_w.shape[0] == 2

    nb = -(-(b + 1) // SUBLANES) * SUBLANES
    c_all = jnp.concatenate([c, c_ctx[None, :], jnp.zeros((nb - b - 1, d), F32)], axis=0)
    mod = _ada(c_all, ada_w, ada_b).reshape(DEPTH, nb, 6, 1, d)

    def lat_idx(tokens_per_row):
        return lambda i, tm: (i * tm) // tokens_per_row

    lat_mod = lat_idx(ll)
    ctx_mod = lambda i, tm: b

    h_lat = x.reshape(tl, d)
    h_ctx = ctx.reshape(tc, d)

    def peer_weights(i):
        wq_t = peer_w_query[i].T.astype(BF16)
        return (wq_t, peer_keys_a[i].astype(BF16), peer_keys_b[i].astype(BF16),
                peer_u[i].astype(BF16), peer_v[i].T.astype(BF16))

    m0 = [mod[0, :, k] for k in range(6)]
    hk = gla_gate_fwd_b.shape[2]
    hv = (gla_w_in.shape[2] - 2 * hk) // 2
    w_in = gla_w_in[0].astype(BF16)
    ga = jnp.concatenate([gla_gate_fwd_a[0], gla_gate_bwd_a[0]], axis=1).astype(BF16)
    zr = jnp.zeros((GLA_GATE_RANK, hk), F32)
    gb = jnp.concatenate([jnp.concatenate([gla_gate_fwd_b[0], zr], axis=1),
                          jnp.concatenate([zr, gla_gate_bwd_b[0]], axis=1)], axis=0).astype(BF16)
    gbias = jnp.concatenate([gla_gate_fwd_bias[0], gla_gate_bwd_bias[0]])[None, :]
    proj_l = _gla_inproj(h_lat, m0[0], m0[1], lat_mod, w_in, ga, gb, gbias, hk, hv)
    proj_c = _gla_inproj(h_ctx, m0[0], m0[1], ctx_mod, w_in, ga, gb, gbias, hk, hv)
    on_c, on_l = _gla_scan(proj_c, proj_l, gla_norm_g[0], b, lc, ll, hk, hv)
    w_out = gla_w_out[0].astype(BF16)
    h_lat = _mix_out(on_l.reshape(tl, hv), proj_l[3], h_lat, m0[2], lat_mod, w_out, ln_tm_g[0], ln_tm_b[0])
    h_ctx = _mix_out(on_c.reshape(tc, hv), proj_c[3], h_ctx, m0[2], ctx_mod, w_out, ln_tm_g[0], ln_tm_b[0])
    pw = peer_weights(0)
    h_lat = _peer(h_lat, m0[3], m0[4], m0[5], lat_mod, *pw, ln_cm_g[0], ln_cm_b[0])
    h_ctx = _peer(h_ctx, m0[3], m0[4], m0[5], ctx_mod, *pw, ln_cm_g[0], ln_cm_b[0])

    m1 = [mod[1, :, k] for k in range(6)]
    nh = MLA_HEADS
    wd = mla_w_down[0]
    o = MLA_Q_RANK + MLA_KV_RANK
    wd = jnp.concatenate([wd, _swap_halves_cols(wd[:, o:])], axis=1).astype(BF16)
    wuq = mla_w_uq[0].reshape(MLA_Q_RANK, nh, MLA_NOPE + MLA_ROPE)
    wuq_rope = wuq[:, :, MLA_NOPE:]
    wuq = jnp.concatenate([wuq[:, :, :MLA_NOPE].reshape(MLA_Q_RANK, -1),
                           wuq_rope.reshape(MLA_Q_RANK, -1),
                           _swap_halves_cols(wuq_rope).reshape(MLA_Q_RANK, -1)], axis=1).astype(BF16)
    wukv = mla_w_ukv[0].reshape(MLA_KV_RANK, nh, MLA_NOPE + MLA_V)
    wukv = jnp.concatenate([wukv[:, :, :MLA_NOPE].reshape(MLA_KV_RANK, -1),
                            wukv[:, :, MLA_NOPE:].reshape(MLA_KV_RANK, -1)], axis=1).astype(BF16)
    qg = mla_q_norm_g[0][None, :]
    kvg = mla_kv_norm_g[0][None, :]
    cos_q, sin_q = _rope_tables(ll, nh)
    cos_k, sin_k = _rope_tables(ll, 1)
    q_l, k_l, v_l = _mla_proj(h_lat.reshape(b, ll, d), m1[0], m1[1], lambda i: i, wd, qg, kvg, wuq, wukv,
                              (cos_q, sin_q, cos_k, sin_k), True)
    k_c, v_c = _mla_proj(h_ctx.reshape(b, lc, d), m1[0], m1[1], lambda i: b, wd, qg, kvg, wuq, wukv,
                         None, False)
    o_l = _mla_attn(q_l, k_c, k_l, v_c, v_l)
    h_lat = _mix_out(o_l.reshape(tl, nh * MLA_V), None, h_lat, m1[2], lat_mod, mla_w_out[0].astype(BF16),
                     ln_tm_g[1], ln_tm_b[1])
    pw = peer_weights(1)
    h_lat = _peer(h_lat, m1[3], m1[4], m1[5], lat_mod, *pw, ln_cm_g[1], ln_cm_b[1])
    return h_lat.reshape(b, ll, d)
```

```python
import functools
import math

import jax
import jax.numpy as jnp
from jax import lax
from jax.experimental import pallas as pl
from jax.experimental.pallas import tpu as pltpu

F32 = jnp.float32
BF16 = jnp.bfloat16
NEG_INF = float("-inf")

DEPTH = 2
ALPHA = (2.0 * DEPTH) ** 0.25
LN_EPS = 1e-5
RMS_EPS = 1e-6

GLA_HEADS = 4
GLA_TAU = 16.0
GLA_CHUNK = 64
GLA_GATE_RANK = 16

MLA_HEADS = 8
MLA_NOPE = 128
MLA_ROPE = 64
MLA_V = 128
MLA_Q_RANK = 256
MLA_KV_RANK = 128
MLA_QK_PAD = 256
GRID_W = 64
ROPE_BASE = 10000.0

PEER_HEADS = 8
PEER_NKEYS = 128
PEER_TOPK = 16
PEER_NEXT = PEER_TOPK + 1

LANES = 128
SUBLANES = 8
VMEM_LIMIT = 56 * 1024 * 1024


def _cparams(sem, flags=None):
    return pltpu.CompilerParams(dimension_semantics=sem, vmem_limit_bytes=VMEM_LIMIT, flags=flags)


def _layer_norm(z, g, b):
    mu = jnp.mean(z, axis=-1, keepdims=True)
    zc = z - mu
    var = jnp.mean(zc * zc, axis=-1, keepdims=True)
    return zc * lax.rsqrt(var + LN_EPS) * g + b


def _silu(x):
    return x / (1.0 + jnp.exp(-x))


def _tile(n, pref):
    t = pref
    while n % t:
        t //= 2
    return t


def _ada_kernel(c_ref, w_ref, b_ref, o_ref):
    s = _silu(c_ref[...])
    o_ref[0] = jnp.dot(s, w_ref[0], preferred_element_type=F32,
                       precision=lax.Precision.HIGHEST) + b_ref[0]


def _ada(c_all, ada_w, ada_b):
    nb, d = c_all.shape
    depth, _, n6 = ada_w.shape
    tn = 1024
    return pl.pallas_call(
        _ada_kernel,
        grid=(depth, n6 // tn),
        in_specs=[pl.BlockSpec((nb, d), lambda i, j: (0, 0)),
                  pl.BlockSpec((1, d, tn), lambda i, j: (i, 0, j)),
                  pl.BlockSpec((1, 1, tn), lambda i, j: (i, 0, j))],
        out_specs=pl.BlockSpec((1, nb, tn), lambda i, j: (i, 0, j)),
        out_shape=jax.ShapeDtypeStruct((depth, nb, n6), F32),
        compiler_params=_cparams(("arbitrary", "arbitrary")),
        name="ada_mod",
    )(c_all, ada_w, ada_b.reshape(depth, 1, n6))


def _gla_inproj_kernel(h_ref, sh_ref, sc_ref, w_ref, ga_ref, gb_ref, gbias_ref,
                       q_ref, k_ref, v_ref, r_ref, lf_ref, lb_ref, *, hk, hv, dk):
    x = h_ref[...] * (1.0 + sc_ref[0]) + sh_ref[0]
    xb = x.astype(BF16)
    y = jnp.dot(xb, w_ref[...], preferred_element_type=F32)
    q_ref[...] = y[:, :hk] * (dk ** -0.5)
    k_ref[...] = y[:, hk:2 * hk]
    v_ref[...] = y[:, 2 * hk:2 * hk + hv]
    r_ref[...] = y[:, 2 * hk + hv:]
    g1 = jnp.dot(xb, ga_ref[...], preferred_element_type=F32)
    z = jnp.dot(g1.astype(BF16), gb_ref[...], preferred_element_type=F32) + gbias_ref[...]
    ls = (jnp.minimum(z, 0.0) - jnp.log1p(jnp.exp(-jnp.abs(z)))) * (1.0 / GLA_TAU)
    lf_ref[...] = ls[:, :hk]
    lb_ref[...] = ls[:, hk:]


def _gla_inproj(h2, sh, sc, mod_idx, w_in, ga, gb, gbias, hk, hv):
    t, d = h2.shape
    tm = _tile(t, 256)
    row = lambda i: (i, 0)
    const = lambda i: (0, 0)
    kern = functools.partial(_gla_inproj_kernel, hk=hk, hv=hv, dk=hk // GLA_HEADS)
    return pl.pallas_call(
        kern,
        grid=(t // tm,),
        in_specs=[pl.BlockSpec((tm, d), row),
                  pl.BlockSpec((1, 1, d), lambda i: (mod_idx(i, tm), 0, 0)),
                  pl.BlockSpec((1, 1, d), lambda i: (mod_idx(i, tm), 0, 0)),
                  pl.BlockSpec(w_in.shape, const),
                  pl.BlockSpec(ga.shape, const),
                  pl.BlockSpec(gb.shape, const),
                  pl.BlockSpec(gbias.shape, const)],
        out_specs=[pl.BlockSpec((tm, hk), row), pl.BlockSpec((tm, hk), row),
                   pl.BlockSpec((tm, hv), row), pl.BlockSpec((tm, hv), row),
                   pl.BlockSpec((tm, hk), row), pl.BlockSpec((tm, hk), row)],
        out_shape=[jax.ShapeDtypeStruct((t, hk), F32), jax.ShapeDtypeStruct((t, hk), F32),
                   jax.ShapeDtypeStruct((t, hv), F32), jax.ShapeDtypeStruct((t, hv), F32),
                   jax.ShapeDtypeStruct((t, hk), F32), jax.ShapeDtypeStruct((t, hk), F32)],
        compiler_params=_cparams(("arbitrary",)),
        name="gla_inproj",
    )(h2, sh, sc, w_in, ga, gb, gbias)


def _split3(x):
    x1 = x.astype(BF16)
    r1 = x - x1.astype(F32)
    x2 = r1.astype(BF16)
    x3 = (r1 - x2.astype(F32)).astype(BF16)
    return x1, x2, x3


def _gla_chunk(q, k, v, lg, st_ref, tri, mask, last_row):
    l1, l2, l3 = _split3(lg)
    cum = (jnp.dot(tri, l1, preferred_element_type=F32)
           + jnp.dot(tri, l2, preferred_element_type=F32)
           + jnp.dot(tri, l3, preferred_element_type=F32))
    last = cum[last_row:last_row + 1, :]
    qe = (q * jnp.exp(cum)).astype(BF16)
    ke = (k * jnp.exp(-cum)).astype(BF16)
    kd = (k * jnp.exp(last - cum)).astype(BF16)
    vb = v.astype(BF16)
    a = lax.dot_general(qe, ke, (((1,), (1,)), ((), ())), preferred_element_type=F32)
    a = jnp.where(mask, a, 0.0)
    st = st_ref[...]
    o = jnp.dot(a.astype(BF16), vb, preferred_element_type=F32)
    o = o + lax.dot_general(qe, st.astype(BF16), (((1,), (1,)), ((), ())),
                            preferred_element_type=F32)
    kvt = lax.dot_general(vb, kd, (((0,), (0,)), ((), ())), preferred_element_type=F32)
    st_ref[...] = st * jnp.exp(last) + kvt
    return o


def _gla_scan_kernel(qc_ref, kc_ref, vc_ref, lfc_ref, lbc_ref,
                     ql_ref, kl_ref, vl_ref, lfl_ref, lbl_ref, ng_ref,
                     oc_ref, ol_ref, *st_refs, dk, dv):
    c = GLA_CHUNK
    nh = len(st_refs) // 2
    ii = lax.broadcasted_iota(jnp.int32, (c, c), 0)
    jj = lax.broadcasted_iota(jnp.int32, (c, c), 1)
    mask_f = ii >= jj
    mask_b = ii < jj
    tri_f = jnp.where(mask_f, 1.0, 0.0).astype(BF16)
    tri_b = jnp.where(ii <= jj, 1.0, 0.0).astype(BF16)

    for st in st_refs:
        st[...] = jnp.zeros_like(st)
    oc_ref[...] = jnp.zeros_like(oc_ref)
    ol_ref[...] = jnp.zeros_like(ol_ref)

    def run(q_ref, k_ref, v_ref, lf_ref, lb_ref, o_ref):
        n = q_ref.shape[1] // c

        def body(i, carry):
            sf = pl.ds(pl.multiple_of(i * c, c), c)
            sb = pl.ds(pl.multiple_of((n - 1 - i) * c, c), c)
            for hh in range(nh):
                ck = slice(hh * dk, (hh + 1) * dk)
                cv = slice(hh * dv, (hh + 1) * dv)
                of = _gla_chunk(q_ref[0, sf, ck], k_ref[0, sf, ck], v_ref[0, sf, cv], lf_ref[0, sf, ck],
                                st_refs[2 * hh], tri_f, mask_f, c - 1)
                o_ref[0, sf, cv] += of
                ob = _gla_chunk(q_ref[0, sb, ck], k_ref[0, sb, ck], v_ref[0, sb, cv], lb_ref[0, sb, ck],
                                st_refs[2 * hh + 1], tri_b, mask_b, 0)
                o_ref[0, sb, cv] += ob
            return carry

        lax.fori_loop(0, n, body, 0)

    run(qc_ref, kc_ref, vc_ref, lfc_ref, lbc_ref, oc_ref)
    run(ql_ref, kl_ref, vl_ref, lfl_ref, lbl_ref, ol_ref)

    g = ng_ref[...]

    def norm(o_ref):
        rows = o_ref.shape[1]
        blk = _tile(rows, 256)

        def body(i, carry):
            s = pl.ds(pl.multiple_of(i * blk, blk), blk)
            for hh in range(nh):
                cv = slice(hh * dv, (hh + 1) * dv)
                o = o_ref[0, s, cv]
                o_ref[0, s, cv] = o * lax.rsqrt(jnp.mean(o * o, axis=-1, keepdims=True) + RMS_EPS) * g
            return carry

        lax.fori_loop(0, rows // blk, body, 0)

    norm(oc_ref)
    norm(ol_ref)


def _gla_scan(proj_c, proj_l, norm_g, b, lc, ll, hk, hv):
    dk, dv = hk // GLA_HEADS, hv // GLA_HEADS
    qc, kc, vc, _, lfc, lbc = [a.reshape(b, lc, -1) for a in proj_c]
    ql, kl, vl, _, lfl, lbl = [a.reshape(b, ll, -1) for a in proj_l]
    hp = 2
    head = lambda i, h: (i, 0, h)
    specs = []
    for n in (lc, ll):
        specs += [pl.BlockSpec((1, n, hp * dk), head), pl.BlockSpec((1, n, hp * dk), head),
                  pl.BlockSpec((1, n, hp * dv), head), pl.BlockSpec((1, n, hp * dk), head),
                  pl.BlockSpec((1, n, hp * dk), head)]
    specs.append(pl.BlockSpec((1, dv), lambda i, h: (0, 0)))
    oc, ol = pl.pallas_call(
        functools.partial(_gla_scan_kernel, dk=dk, dv=dv),
        grid=(b, GLA_HEADS // hp),
        in_specs=specs,
        out_specs=[pl.BlockSpec((1, lc, hp * dv), head), pl.BlockSpec((1, ll, hp * dv), head)],
        out_shape=[jax.ShapeDtypeStruct((b, lc, hv), F32), jax.ShapeDtypeStruct((b, ll, hv), F32)],
        scratch_shapes=[pltpu.VMEM((dv, dk), F32) for _ in range(2 * hp)],
        compiler_params=_cparams(("arbitrary", "arbitrary")),
        name="gla_scan",
    )(qc, kc, vc, lfc, lbc, ql, kl, vl, lfl, lbl, norm_g.reshape(1, dv))
    return oc, ol


def _mix_out_kernel(*refs, gated):
    if gated:
        o_ref, r_ref, h_ref, g_ref, w_ref, lng_ref, lnb_ref, out_ref = refs
        o = o_ref[...] * _silu(r_ref[...])
    else:
        o_ref, h_ref, g_ref, w_ref, lng_ref, lnb_ref, out_ref = refs
        o = o_ref[...]
    y = jnp.dot(o.astype(BF16), w_ref[...], preferred_element_type=F32)
    z = ALPHA * h_ref[...] + g_ref[0] * y
    out_ref[...] = _layer_norm(z, lng_ref[...], lnb_ref[...])


def _mix_out(o2, r2, h2, gate, mod_idx, w_out, ln_g, ln_b):
    t, d = h2.shape
    kdim = o2.shape[1]
    tm = _tile(t, 512)
    row = lambda i: (i, 0)
    const = lambda i: (0, 0)
    gated = r2 is not None
    ins = [o2] + ([r2] if gated else []) + [h2, gate, w_out, ln_g.reshape(1, d), ln_b.reshape(1, d)]
    specs = [pl.BlockSpec((tm, kdim), row)] + ([pl.BlockSpec((tm, kdim), row)] if gated else [])
    specs += [pl.BlockSpec((tm, d), row),
              pl.BlockSpec((1, 1, d), lambda i: (mod_idx(i, tm), 0, 0)),
              pl.BlockSpec(w_out.shape, const),
              pl.BlockSpec((1, d), const), pl.BlockSpec((1, d), const)]
    return pl.pallas_call(
        functools.partial(_mix_out_kernel, gated=gated),
        grid=(t // tm,),
        in_specs=specs,
        out_specs=pl.BlockSpec((tm, d), row),
        out_shape=jax.ShapeDtypeStruct((t, d), F32),
        compiler_params=_cparams(("arbitrary",)),
        name="mix_out",
    )(*ins)


def _oddeven_merge_sort_pairs(n):
    pairs = []
    p = 1
    while p < n:
        k = p
        while k >= 1:
            for j in range(k % p, n - k, 2 * k):
                for i in range(min(k, n - j - k)):
                    if (i + j) // (2 * p) == (i + j + k) // (2 * p):
                        pairs.append((i + j, i + j + k))
            k //= 2
        p *= 2
    return pairs


def _sublane_allmax(x):
    return jnp.broadcast_to(jnp.max(x, axis=0, keepdims=True), x.shape)


def _merge_top(rows, n):
    rows = list(rows)
    out = []
    for k in range(n):
        head = rows[0]
        m = _sublane_allmax(head)
        out.append(m)
        rem = n - 1 - k
        if rem == 0:
            break
        hit = head >= m
        for r in range(min(rem, len(rows))):
            nxt = rows[r + 1] if r + 1 < len(rows) else NEG_INF
            rows[r] = jnp.where(hit, nxt, rows[r])
    return out


def _peer_route(sa, sb):
    n = PEER_NEXT
    nr = PEER_NKEYS // SUBLANES

    def top(s):
        rows = [s[r * SUBLANES:(r + 1) * SUBLANES, :] for r in range(nr)]
        for i, j in _oddeven_merge_sort_pairs(nr):
            rows[i], rows[j] = jnp.maximum(rows[i], rows[j]), jnp.minimum(rows[i], rows[j])
        return _merge_top(rows, n)

    va = top(sa)
    vb = top(sb)
    sub = lax.broadcasted_iota(jnp.int32, (SUBLANES, LANES), 0)
    nj = 4
    lens = [n // (j + 1) for j in range(nj)]
    lens += [sum(1 for j in range(nj, n) if (i + 1) * (j + 1) <= n) for i in range(SUBLANES - nj)]
    assert sum(lens) == sum(n // (j + 1) for j in range(n)) and lens[-1] == 0
    len_t = jnp.zeros((SUBLANES, LANES), jnp.int32)
    vb_lo = vb[0]
    va_hi = va[0]
    for s in range(SUBLANES):
        len_t = jnp.where(sub == s, lens[s], len_t)
        if 0 < s < nj:
            vb_lo = jnp.where(sub == s, vb[s], vb_lo)
        if s > nj:
            va_hi = jnp.where(sub == s, va[s - nj], va_hi)
    cand = []
    for r in range(max(lens)):
        row = jnp.where(sub < nj, va[r] + vb_lo, va_hi + vb[min(nj + r, n - 1)])
        cand.append(jnp.where(len_t > r, row, NEG_INF))
    e = _merge_top(cand, n)
    tau = 0.5 * (e[PEER_TOPK - 1] + e[PEER_TOPK])
    z = jnp.ones_like(e[0])
    for k in range(1, PEER_TOPK):
        z = z + jnp.exp(e[k] - e[0])
    zinv = 1.0 / z
    tb = tau - vb[0]
    ea, eb, th = [], [], []
    for r in range(nr):
        sar = sa[r * SUBLANES:(r + 1) * SUBLANES, :]
        sbr = sb[r * SUBLANES:(r + 1) * SUBLANES, :]
        ea.append(jnp.exp(sar - va[0]) * zinv)
        eb.append(jnp.exp(sbr - vb[0]))
        th.append(jnp.exp(tb - sar))
    return (jnp.concatenate(ea, axis=0), jnp.concatenate(eb, axis=0), jnp.concatenate(th, axis=0))


def _peer_build_rows(p_ref, ht_ref, ea_ref, eb_ref, th_ref, arow0, ar, al, live):
    nk = PEER_NKEYS
    nlt = p_ref.shape[1] // LANES
    arows = pl.ds(pl.multiple_of(arow0, SUBLANES), SUBLANES)
    r = slice(al * nk, (al + 1) * nk)
    for lt in range(nlt):
        ls = slice(lt * LANES, (lt + 1) * LANES)
        w = jnp.zeros((nk, LANES), F32)
        for hd in range(PEER_HEADS):
            thr = th_ref[hd, arows, ls][ar:ar + 1, :]
            ear = ea_ref[hd, arows, ls][ar:ar + 1, :]
            if live is not None:
                ear = jnp.where(live, ear, 0.0)
            ebv = eb_ref[hd, :, ls]
            w = w + jnp.where(ebv >= thr, ebv, 0.0) * ear
        hh = ht_ref[r, ls]
        gl = 0.5 * hh * (1.0 + lax.erf(hh * (2.0 ** -0.5)))
        p_ref[r, ls] = (w * gl).astype(BF16)


def _peer_kernel(hp_ref, sh_ref, sc_ref, he_ref, gate_ref, wq_ref, ka_ref, kb_ref, u_ref, vt_ref,
                 lng_ref, lnb_ref, out_ref,
                 xt_s, q_s, ea_s, eb_s, th_s, ht0_s, ht1_s, p0_s, p1_s, yt_s, *, n_tiles):
    g = pl.program_id(0)
    tm = hp_ref.shape[0]
    nlt = tm // LANES
    nk = PEER_NKEYS
    half = u_ref.shape[0] // 2
    a_half = half // nk
    pairs = nk // (2 * a_half)
    tile = g // pairs
    jp = g % pairs
    par = tile % 2
    gm1 = jnp.maximum(g - 1, 0)
    tile_prev = gm1 // pairs
    par_prev = tile_prev % 2
    jp_prev = gm1 % pairs

    @pl.when(g == 0)
    def _init():
        for ref in (ht0_s, ht1_s, p0_s, p1_s, yt_s, ea_s, eb_s, th_s):
            ref[...] = jnp.zeros_like(ref)

    @pl.when((jp == 0) & (tile < n_tiles))
    def _prologue():
        x = hp_ref[...] * (1.0 + sc_ref[0]) + sh_ref[0]
        xt_s[par] = x.T.astype(BF16)

        def head_body(hd, carry):
            rq = pl.ds(pl.multiple_of(hd * 2 * nk, 2 * nk), 2 * nk)
            q_s[...] = jnp.dot(wq_ref[rq, :], xt_s[par], preferred_element_type=F32)
            sa = jnp.dot(ka_ref[hd], q_s[0:nk, :].astype(BF16), preferred_element_type=F32)
            sb = jnp.dot(kb_ref[hd], q_s[nk:, :].astype(BF16), preferred_element_type=F32)
            for lt in range(nlt):
                ls = slice(lt * LANES, (lt + 1) * LANES)
                ea, eb, th = _peer_route(sa[:, ls], sb[:, ls])
                ea_s[par, hd, :, ls] = ea
                eb_s[par, hd, :, ls] = eb
                th_s[par, hd, :, ls] = th
            return carry

        lax.fori_loop(0, PEER_HEADS, head_body, 0)

    xt = xt_s.at[par]
    prev = (ea_s.at[par_prev], eb_s.at[par_prev], th_s.at[par_prev], jp_prev * 2 * a_half)
    cur = (ea_s.at[par], eb_s.at[par], th_s.at[par], jp * 2 * a_half)
    dq = yt_s.shape[0] // a_half
    for k in range(a_half):
        rk = slice(k * nk, (k + 1) * nk)
        rd = slice(k * dq, (k + 1) * dq)
        ht0_s[rk, :] = jnp.dot(u_ref[rk, :], xt[...], preferred_element_type=F32)
        yt_s[rd, :] += jnp.dot(vt_ref[rd, 0:half], p0_s[...], preferred_element_type=F32)
        _peer_build_rows(p1_s, ht1_s, *prev, a_half + k, k, g > 0)
    @pl.when(jp >= 0)
    def _half_b():
        for k in range(a_half):
            rk = slice(k * nk, (k + 1) * nk)
            rd = slice(k * dq, (k + 1) * dq)
            ht1_s[rk, :] = jnp.dot(u_ref[half + k * nk:half + (k + 1) * nk, :], xt[...],
                                   preferred_element_type=F32)
            yt_s[rd, :] += jnp.dot(vt_ref[rd, half:], p1_s[...], preferred_element_type=F32)
            _peer_build_rows(p0_s, ht0_s, *cur, k, k, None)

    @pl.when((jp == 0) & (g > 0))
    def _epilogue():
        y = yt_s[...].T
        z = ALPHA * he_ref[...] + gate_ref[0] * y
        out_ref[...] = _layer_norm(z, lng_ref[...], lnb_ref[...])
        yt_s[...] = jnp.zeros_like(yt_s)


def _peer(h2, sh, sc, gate, mod_idx, wq_t, ka, kb, u_b, vt_b, ln_g, ln_b):
    t, d = h2.shape
    ne = u_b.shape[0]
    tm = _tile(t, 512)
    n_tiles = t // tm
    ce = SUBLANES * PEER_NKEYS
    half = ce // 2
    pairs = ne // ce
    const2 = lambda g: (0, 0)
    const3 = lambda g: (0, 0, 0)
    pro_tile = lambda g: jnp.minimum(g // pairs, n_tiles - 1)
    epi_tile = lambda g: jnp.maximum(g - 1, 0) // pairs
    factors = (2, PEER_HEADS, PEER_NKEYS, tm)
    return pl.pallas_call(
        functools.partial(_peer_kernel, n_tiles=n_tiles),
        grid=(n_tiles * pairs + 1,),
        in_specs=[pl.BlockSpec((tm, d), lambda g: (pro_tile(g), 0)),
                  pl.BlockSpec((1, 1, d), lambda g: (mod_idx(pro_tile(g), tm), 0, 0)),
                  pl.BlockSpec((1, 1, d), lambda g: (mod_idx(pro_tile(g), tm), 0, 0)),
                  pl.BlockSpec((tm, d), lambda g: (epi_tile(g), 0)),
                  pl.BlockSpec((1, 1, d), lambda g: (mod_idx(epi_tile(g), tm), 0, 0)),
                  pl.BlockSpec(wq_t.shape, const2),
                  pl.BlockSpec(ka.shape, const3),
                  pl.BlockSpec(kb.shape, const3),
                  pl.BlockSpec((ce, d), lambda g: (g % pairs, 0)),
                  pl.BlockSpec((d, ce), lambda g: (0, (g + pairs - 1) % pairs)),
                  pl.BlockSpec((1, d), const2), pl.BlockSpec((1, d), const2)],
        out_specs=pl.BlockSpec((tm, d), lambda g: (epi_tile(g), 0)),
        out_shape=jax.ShapeDtypeStruct((t, d), F32),
        scratch_shapes=[pltpu.VMEM((2, d, tm), BF16),
                        pltpu.VMEM((2 * PEER_NKEYS, tm), F32),
                        pltpu.VMEM(factors, F32), pltpu.VMEM(factors, F32), pltpu.VMEM(factors, F32),
                        pltpu.VMEM((half, tm), F32), pltpu.VMEM((half, tm), F32),
                        pltpu.VMEM((half, tm), BF16), pltpu.VMEM((half, tm), BF16),
                        pltpu.VMEM((d, tm), F32)],
        compiler_params=_cparams(("arbitrary",), None),
        name="peer",
    )(h2, sh, sc, h2, gate, wq_t, ka, kb, u_b, vt_b, ln_g.reshape(1, d), ln_b.reshape(1, d))


def _mla_proj_kernel(*refs, rotate, want_q):
    if rotate:
        (h_ref, sh_ref, sc_ref, wd_ref, qg_ref, kvg_ref, wuq_ref, wukv_ref,
         cq_ref, sq_ref, ck_ref, sk_ref) = refs[:12]
        outs = refs[12:]
    else:
        h_ref, sh_ref, sc_ref, wd_ref, qg_ref, kvg_ref, wuq_ref, wukv_ref = refs[:8]
        outs = refs[8:]
    if want_q:
        q_ref, k_ref, v_ref = outs
    else:
        k_ref, v_ref = outs
    nh = MLA_HEADS
    x = h_ref[0] * (1.0 + sc_ref[0]) + sh_ref[0]
    dn = jnp.dot(x.astype(BF16), wd_ref[...], preferred_element_type=F32)
    cq = dn[:, :MLA_Q_RANK]
    ckv = dn[:, MLA_Q_RANK:MLA_Q_RANK + MLA_KV_RANK]
    o = MLA_Q_RANK + MLA_KV_RANK
    kr = dn[:, o:o + MLA_ROPE]
    kr_sw = dn[:, o + MLA_ROPE:o + 2 * MLA_ROPE]
    if rotate:
        kr = kr * ck_ref[...] + kr_sw * sk_ref[...]
    ckvn = ckv * lax.rsqrt(jnp.mean(ckv * ckv, axis=-1, keepdims=True) + RMS_EPS) * kvg_ref[...]
    kv = jnp.dot(ckvn.astype(BF16), wukv_ref[...], preferred_element_type=F32)
    tm = x.shape[0]
    zpad = jnp.zeros((tm, MLA_QK_PAD - MLA_NOPE - MLA_ROPE), F32)
    krp = jnp.concatenate([kr, zpad], axis=-1).astype(BF16)
    for hd in range(nh):
        k_ref[0, hd, :, 0:MLA_NOPE] = kv[:, hd * MLA_NOPE:(hd + 1) * MLA_NOPE].astype(BF16)
        k_ref[0, hd, :, MLA_NOPE:] = krp
        v_ref[0, hd] = kv[:, nh * MLA_NOPE + hd * MLA_V:nh * MLA_NOPE + (hd + 1) * MLA_V].astype(BF16)
    if want_q:
        cqn = cq * lax.rsqrt(jnp.mean(cq * cq, axis=-1, keepdims=True) + RMS_EPS) * qg_ref[...]
        q = jnp.dot(cqn.astype(BF16), wuq_ref[...], preferred_element_type=F32)
        scale = (MLA_NOPE + MLA_ROPE) ** -0.5
        base = nh * MLA_NOPE
        qr = q[:, base:base + nh * MLA_ROPE]
        if rotate:
            qr_sw = q[:, base + nh * MLA_ROPE:]
            qr = qr * cq_ref[...] + qr_sw * sq_ref[...]
        for hd in range(nh):
            q_ref[0, hd, :, 0:MLA_NOPE] = (q[:, hd * MLA_NOPE:(hd + 1) * MLA_NOPE] * scale).astype(BF16)
            qrp = jnp.concatenate([qr[:, hd * MLA_ROPE:(hd + 1) * MLA_ROPE] * scale, zpad], axis=-1)
            q_ref[0, hd, :, MLA_NOPE:] = qrp.astype(BF16)


def _mla_proj(h3, sh, sc, mod_idx, wd, qg, kvg, wuq, wukv, rope, want_q):
    b, n, d = h3.shape
    tm = _tile(n, 256)
    nt = n // tm
    rotate = rope is not None
    tok = lambda i, t: (i, t, 0)
    const = lambda i, t: (0, 0)
    mod = lambda i, t: (mod_idx(i), 0, 0)
    ins = [h3, sh, sc, wd, qg, kvg, wuq, wukv]
    specs = [pl.BlockSpec((1, tm, d), tok), pl.BlockSpec((1, 1, d), mod), pl.BlockSpec((1, 1, d), mod),
             pl.BlockSpec(wd.shape, const), pl.BlockSpec(qg.shape, const), pl.BlockSpec(kvg.shape, const),
             pl.BlockSpec(wuq.shape, const), pl.BlockSpec(wukv.shape, const)]
    if rotate:
        ins += list(rope)
        specs += [pl.BlockSpec((tm, a.shape[1]), lambda i, t: (t, 0)) for a in rope]
    hd4 = lambda i, t: (i, 0, t, 0)
    out_specs, out_shape = [], []
    if want_q:
        out_specs.append(pl.BlockSpec((1, MLA_HEADS, tm, MLA_QK_PAD), hd4))
        out_shape.append(jax.ShapeDtypeStruct((b, MLA_HEADS, n, MLA_QK_PAD), BF16))
    out_specs += [pl.BlockSpec((1, MLA_HEADS, tm, MLA_QK_PAD), hd4),
                  pl.BlockSpec((1, MLA_HEADS, tm, MLA_V), hd4)]
    out_shape += [jax.ShapeDtypeStruct((b, MLA_HEADS, n, MLA_QK_PAD), BF16),
                  jax.ShapeDtypeStruct((b, MLA_HEADS, n, MLA_V), BF16)]

    return pl.pallas_call(
        functools.partial(_mla_proj_kernel, rotate=rotate, want_q=want_q),
        grid=(b, nt),
        in_specs=specs,
        out_specs=out_specs,
        out_shape=out_shape,
        compiler_params=_cparams(("arbitrary", "arbitrary")),
        name="mla_proj_q" if want_q else "mla_proj_kv",
    )(*ins)


def _mla_attn_kernel(q_ref, kc_ref, kl_ref, vc_ref, vl_ref, o_ref):
    nt = (((1,), (1,)), ((), ()))
    for hh in range(q_ref.shape[1]):
        q = q_ref[0, hh]
        s_c = lax.dot_general(q, kc_ref[0, hh], nt, preferred_element_type=F32)
        s_l = lax.dot_general(q, kl_ref[0, hh], nt, preferred_element_type=F32)
        m = jnp.maximum(jnp.max(s_c, axis=-1, keepdims=True), jnp.max(s_l, axis=-1, keepdims=True))
        p_c = jnp.exp(s_c - m)
        p_l = jnp.exp(s_l - m)
        den = jnp.sum(p_c, axis=-1, keepdims=True) + jnp.sum(p_l, axis=-1, keepdims=True)
        o = (jnp.dot(p_c.astype(BF16), vc_ref[0, hh], preferred_element_type=F32)
             + jnp.dot(p_l.astype(BF16), vl_ref[0, hh], preferred_element_type=F32))
        o_ref[0, :, hh * MLA_V:(hh + 1) * MLA_V] = (o / den).astype(o_ref.dtype)


def _mla_attn(q, kc, kl, vc, vl):
    b, nh, n, dq = q.shape
    lc = kc.shape[2]
    tq = _tile(n, 256)
    hp = 4
    kv = lambda i, h, t: (i, h, 0, 0)
    return pl.pallas_call(
        _mla_attn_kernel,
        grid=(b, nh // hp, n // tq),
        in_specs=[pl.BlockSpec((1, hp, tq, dq), lambda i, h, t: (i, h, t, 0)),
                  pl.BlockSpec((1, hp, lc, dq), kv), pl.BlockSpec((1, hp, n, dq), kv),
                  pl.BlockSpec((1, hp, lc, MLA_V), kv), pl.BlockSpec((1, hp, n, MLA_V), kv)],
        out_specs=pl.BlockSpec((1, tq, hp * MLA_V), lambda i, h, t: (i, t, h)),
        out_shape=jax.ShapeDtypeStruct((b, n, nh * MLA_V), BF16),
        compiler_params=_cparams(("arbitrary", "arbitrary", "arbitrary")),
        name="mla_attn",
    )(q, kc, kl, vc, vl)


def _rope_tables(n_tokens, reps):
    rows = n_tokens // GRID_W
    row = jnp.repeat(jnp.arange(rows, dtype=F32), GRID_W)
    col = jnp.tile(jnp.arange(GRID_W, dtype=F32), rows)
    n_freq = MLA_ROPE // 4
    inv_freq = ROPE_BASE ** (-jnp.arange(n_freq, dtype=F32) / n_freq)
    ar = row[:, None] * inv_freq
    ac = col[:, None] * inv_freq
    cos = jnp.concatenate([jnp.cos(ar), jnp.cos(ar), jnp.cos(ac), jnp.cos(ac)], axis=-1)
    sin = jnp.concatenate([-jnp.sin(ar), jnp.sin(ar), -jnp.sin(ac), jnp.sin(ac)], axis=-1)
    return jnp.tile(cos, (1, reps)), jnp.tile(sin, (1, reps))


def _swap_halves_cols(w):
    q = MLA_ROPE // 4
    return jnp.concatenate([w[..., q:2 * q], w[..., 0:q], w[..., 3 * q:4 * q], w[..., 2 * q:3 * q]], axis=-1)


def kernel(x, c, ctx, c_ctx, ada_w, ada_b, ln_tm_g, ln_tm_b, ln_cm_g, ln_cm_b, gla_w_in, gla_gate_fwd_a, gla_gate_fwd_b, gla_gate_fwd_bias, gla_gate_bwd_a, gla_gate_bwd_b, gla_gate_bwd_bias, gla_norm_g, gla_w_out, mla_w_down, mla_q_norm_g, mla_kv_norm_g, mla_w_uq, mla_w_ukv, mla_w_out, peer_w_query, peer_keys_a, peer_keys_b, peer_u, peer_v):
    b, ll, d = x.shape
    lc = ctx.shape[1]
    tl, tc = b * ll, b * lc
    assert DEPTH == ada_w.shape[0] == 2

    nb = -(-(b + 1) // SUBLANES) * SUBLANES
    c_all = jnp.concatenate([c, c_ctx[None, :], jnp.zeros((nb - b - 1, d), F32)], axis=0)
    mod = _ada(c_all, ada_w, ada_b).reshape(DEPTH, nb, 6, 1, d)

    def lat_idx(tokens_per_row):
        return lambda i, tm: (i * tm) // tokens_per_row

    lat_mod = lat_idx(ll)
    ctx_mod = lambda i, tm: b

    h_lat = x.reshape(tl, d)
    h_ctx = ctx.reshape(tc, d)

    def peer_weights(i):
        wq_t = peer_w_query[i].T.astype(BF16)
        return (wq_t, peer_keys_a[i].astype(BF16), peer_keys_b[i].astype(BF16),
                peer_u[i].astype(BF16), peer_v[i].T.astype(BF16))

    m0 = [mod[0, :, k] for k in range(6)]
    hk = gla_gate_fwd_b.shape[2]
    hv = (gla_w_in.shape[2] - 2 * hk) // 2
    w_in = gla_w_in[0].astype(BF16)
    ga = jnp.concatenate([gla_gate_fwd_a[0], gla_gate_bwd_a[0]], axis=1).astype(BF16)
    zr = jnp.zeros((GLA_GATE_RANK, hk), F32)
    gb = jnp.concatenate([jnp.concatenate([gla_gate_fwd_b[0], zr], axis=1),
                          jnp.concatenate([zr, gla_gate_bwd_b[0]], axis=1)], axis=0).astype(BF16)
    gbias = jnp.concatenate([gla_gate_fwd_bias[0], gla_gate_bwd_bias[0]])[None, :]
    proj_l = _gla_inproj(h_lat, m0[0], m0[1], lat_mod, w_in, ga, gb, gbias, hk, hv)
    proj_c = _gla_inproj(h_ctx, m0[0], m0[1], ctx_mod, w_in, ga, gb, gbias, hk, hv)
    on_c, on_l = _gla_scan(proj_c, proj_l, gla_norm_g[0], b, lc, ll, hk, hv)
    w_out = gla_w_out[0].astype(BF16)
    h_lat = _mix_out(on_l.reshape(tl, hv), proj_l[3], h_lat, m0[2], lat_mod, w_out, ln_tm_g[0], ln_tm_b[0])
    h_ctx = _mix_out(on_c.reshape(tc, hv), proj_c[3], h_ctx, m0[2], ctx_mod, w_out, ln_tm_g[0], ln_tm_b[0])
    pw = peer_weights(0)
    h_lat = _peer(h_lat, m0[3], m0[4], m0[5], lat_mod, *pw, ln_cm_g[0], ln_cm_b[0])
    h_ctx = _peer(h_ctx, m0[3], m0[4], m0[5], ctx_mod, *pw, ln_cm_g[0], ln_cm_b[0])

    m1 = [mod[1, :, k] for k in range(6)]
    nh = MLA_HEADS
    wd = mla_w_down[0]
    o = MLA_Q_RANK + MLA_KV_RANK
    wd = jnp.concatenate([wd, _swap_halves_cols(wd[:, o:])], axis=1).astype(BF16)
    wuq = mla_w_uq[0].reshape(MLA_Q_RANK, nh, MLA_NOPE + MLA_ROPE)
    wuq_rope = wuq[:, :, MLA_NOPE:]
    wuq = jnp.concatenate([wuq[:, :, :MLA_NOPE].reshape(MLA_Q_RANK, -1),
                           wuq_rope.reshape(MLA_Q_RANK, -1),
                           _swap_halves_cols(wuq_rope).reshape(MLA_Q_RANK, -1)], axis=1).astype(BF16)
    wukv = mla_w_ukv[0].reshape(MLA_KV_RANK, nh, MLA_NOPE + MLA_V)
    wukv = jnp.concatenate([wukv[:, :, :MLA_NOPE].reshape(MLA_KV_RANK, -1),
                            wukv[:, :, MLA_NOPE:].reshape(MLA_KV_RANK, -1)], axis=1).astype(BF16)
    qg = mla_q_norm_g[0][None, :]
    kvg = mla_kv_norm_g[0][None, :]
    cos_q, sin_q = _rope_tables(ll, nh)
    cos_k, sin_k = _rope_tables(ll, 1)
    q_l, k_l, v_l = _mla_proj(h_lat.reshape(b, ll, d), m1[0], m1[1], lambda i: i, wd, qg, kvg, wuq, wukv,
                              (cos_q, sin_q, cos_k, sin_k), True)
    k_c, v_c = _mla_proj(h_ctx.reshape(b, lc, d), m1[0], m1[1], lambda i: b, wd, qg, kvg, wuq, wukv,
                         None, False)
    o_l = _mla_attn(q_l, k_c, k_l, v_c, v_l)
    h_lat = _mix_out(o_l.reshape(tl, nh * MLA_V), None, h_lat, m1[2], lat_mod, mla_w_out[0].astype(BF16),
                     ln_tm_g[1], ln_tm_b[1])
    pw = peer_weights(1)
    h_lat = _peer(h_lat, m1[3], m1[4], m1[5], lat_mod, *pw, ln_cm_g[1], ln_cm_b[1])
    return h_lat.reshape(b, ll, d)
```

```python
import functools
import math

import jax
import jax.numpy as jnp
from jax import lax
from jax.experimental import pallas as pl
from jax.experimental.pallas import tpu as pltpu

F32 = jnp.float32
BF16 = jnp.bfloat16
NEG_INF = float("-inf")

DEPTH = 2
ALPHA = (2.0 * DEPTH) ** 0.25
LN_EPS = 1e-5
RMS_EPS = 1e-6

GLA_HEADS = 4
GLA_TAU = 16.0
GLA_CHUNK = 64
GLA_GATE_RANK = 16

MLA_HEADS = 8
MLA_NOPE = 128
MLA_ROPE = 64
MLA_V = 128
MLA_Q_RANK = 256
MLA_KV_RANK = 128
MLA_QK_PAD = 256
GRID_W = 64
ROPE_BASE = 10000.0

PEER_HEADS = 8
PEER_NKEYS = 128
PEER_TOPK = 16
PEER_NEXT = PEER_TOPK + 1

LANES = 128
SUBLANES = 8
VMEM_LIMIT = 56 * 1024 * 1024


def _cparams(sem, flags=None):
    return pltpu.CompilerParams(dimension_semantics=sem, vmem_limit_bytes=VMEM_LIMIT, flags=flags)


def _layer_norm(z, g, b):
    mu = jnp.mean(z, axis=-1, keepdims=True)
    zc = z - mu
    var = jnp.mean(zc * zc, axis=-1, keepdims=True)
    return zc * lax.rsqrt(var + LN_EPS) * g + b


def _silu(x):
    return x / (1.0 + jnp.exp(-x))


def _tile(n, pref):
    t = pref
    while n % t:
        t //= 2
    return t


def _ada_kernel(c_ref, w_ref, b_ref, o_ref):
    s = _silu(c_ref[...])
    o_ref[0] = jnp.dot(s, w_ref[0], preferred_element_type=F32,
                       precision=lax.Precision.HIGHEST) + b_ref[0]


def _ada(c_all, ada_w, ada_b):
    nb, d = c_all.shape
    depth, _, n6 = ada_w.shape
    tn = 1024
    return pl.pallas_call(
        _ada_kernel,
        grid=(depth, n6 // tn),
        in_specs=[pl.BlockSpec((nb, d), lambda i, j: (0, 0)),
                  pl.BlockSpec((1, d, tn), lambda i, j: (i, 0, j)),
                  pl.BlockSpec((1, 1, tn), lambda i, j: (i, 0, j))],
        out_specs=pl.BlockSpec((1, nb, tn), lambda i, j: (i, 0, j)),
        out_shape=jax.ShapeDtypeStruct((depth, nb, n6), F32),
        compiler_params=_cparams(("arbitrary", "arbitrary")),
        name="ada_mod",
    )(c_all, ada_w, ada_b.reshape(depth, 1, n6))


def _gla_inproj_kernel(h_ref, sh_ref, sc_ref, w_ref, ga_ref, gb_ref, gbias_ref,
                       q_ref, k_ref, v_ref, r_ref, lf_ref, lb_ref, *, hk, hv, dk):
    x = h_ref[...] * (1.0 + sc_ref[0]) + sh_ref[0]
    xb = x.astype(BF16)
    y = jnp.dot(xb, w_ref[...], preferred_element_type=F32)
    q_ref[...] = y[:, :hk] * (dk ** -0.5)
    k_ref[...] = y[:, hk:2 * hk]
    v_ref[...] = y[:, 2 * hk:2 * hk + hv]
    r_ref[...] = y[:, 2 * hk + hv:]
    g1 = jnp.dot(xb, ga_ref[...], preferred_element_type=F32)
    z = jnp.dot(g1.astype(BF16), gb_ref[...], preferred_element_type=F32) + gbias_ref[...]
    ls = (jnp.minimum(z, 0.0) - jnp.log1p(jnp.exp(-jnp.abs(z)))) * (1.0 / GLA_TAU)
    lf_ref[...] = ls[:, :hk]
    lb_ref[...] = ls[:, hk:]


def _gla_inproj(h2, sh, sc, mod_idx, w_in, ga, gb, gbias, hk, hv):
    t, d = h2.shape
    tm = _tile(t, 256)
    row = lambda i: (i, 0)
    const = lambda i: (0, 0)
    kern = functools.partial(_gla_inproj_kernel, hk=hk, hv=hv, dk=hk // GLA_HEADS)
    return pl.pallas_call(
        kern,
        grid=(t // tm,),
        in_specs=[pl.BlockSpec((tm, d), row),
                  pl.BlockSpec((1, 1, d), lambda i: (mod_idx(i, tm), 0, 0)),
                  pl.BlockSpec((1, 1, d), lambda i: (mod_idx(i, tm), 0, 0)),
                  pl.BlockSpec(w_in.shape, const),
                  pl.BlockSpec(ga.shape, const),
                  pl.BlockSpec(gb.shape, const),
                  pl.BlockSpec(gbias.shape, const)],
        out_specs=[pl.BlockSpec((tm, hk), row), pl.BlockSpec((tm, hk), row),
                   pl.BlockSpec((tm, hv), row), pl.BlockSpec((tm, hv), row),
                   pl.BlockSpec((tm, hk), row), pl.BlockSpec((tm, hk), row)],
        out_shape=[jax.ShapeDtypeStruct((t, hk), F32), jax.ShapeDtypeStruct((t, hk), F32),
                   jax.ShapeDtypeStruct((t, hv), F32), jax.ShapeDtypeStruct((t, hv), F32),
                   jax.ShapeDtypeStruct((t, hk), F32), jax.ShapeDtypeStruct((t, hk), F32)],
        compiler_params=_cparams(("arbitrary",)),
        name="gla_inproj",
    )(h2, sh, sc, w_in, ga, gb, gbias)


def _split3(x):
    x1 = x.astype(BF16)
    r1 = x - x1.astype(F32)
    x2 = r1.astype(BF16)
    x3 = (r1 - x2.astype(F32)).astype(BF16)
    return x1, x2, x3


def _gla_chunk(q, k, v, lg, st_ref, tri, mask, last_row):
    l1, l2, l3 = _split3(lg)
    cum = (jnp.dot(tri, l1, preferred_element_type=F32)
           + jnp.dot(tri, l2, preferred_element_type=F32)
           + jnp.dot(tri, l3, preferred_element_type=F32))
    last = cum[last_row:last_row + 1, :]
    qe = (q * jnp.exp(cum)).astype(BF16)
    ke = (k * jnp.exp(-cum)).astype(BF16)
    kd = (k * jnp.exp(last - cum)).astype(BF16)
    vb = v.astype(BF16)
    a = lax.dot_general(qe, ke, (((1,), (1,)), ((), ())), preferred_element_type=F32)
    a = jnp.where(mask, a, 0.0)
    st = st_ref[...]
    o = jnp.dot(a.astype(BF16), vb, preferred_element_type=F32)
    o = o + lax.dot_general(qe, st.astype(BF16), (((1,), (1,)), ((), ())),
                            preferred_element_type=F32)
    kvt = lax.dot_general(vb, kd, (((0,), (0,)), ((), ())), preferred_element_type=F32)
    st_ref[...] = st * jnp.exp(last) + kvt
    return o


def _gla_scan_kernel(qc_ref, kc_ref, vc_ref, lfc_ref, lbc_ref,
                     ql_ref, kl_ref, vl_ref, lfl_ref, lbl_ref, ng_ref,
                     oc_ref, ol_ref, *st_refs, dk, dv):
    c = GLA_CHUNK
    nh = len(st_refs) // 2
    ii = lax.broadcasted_iota(jnp.int32, (c, c), 0)
    jj = lax.broadcasted_iota(jnp.int32, (c, c), 1)
    mask_f = ii >= jj
    mask_b = ii < jj
    tri_f = jnp.where(mask_f, 1.0, 0.0).astype(BF16)
    tri_b = jnp.where(ii <= jj, 1.0, 0.0).astype(BF16)

    for st in st_refs:
        st[...] = jnp.zeros_like(st)
    oc_ref[...] = jnp.zeros_like(oc_ref)
    ol_ref[...] = jnp.zeros_like(ol_ref)

    def run(q_ref, k_ref, v_ref, lf_ref, lb_ref, o_ref):
        n = q_ref.shape[1] // c

        def body(i, carry):
            sf = pl.ds(pl.multiple_of(i * c, c), c)
            sb = pl.ds(pl.multiple_of((n - 1 - i) * c, c), c)
            for hh in range(nh):
                ck = slice(hh * dk, (hh + 1) * dk)
                cv = slice(hh * dv, (hh + 1) * dv)
                of = _gla_chunk(q_ref[0, sf, ck], k_ref[0, sf, ck], v_ref[0, sf, cv], lf_ref[0, sf, ck],
                                st_refs[2 * hh], tri_f, mask_f, c - 1)
                o_ref[0, sf, cv] += of
                ob = _gla_chunk(q_ref[0, sb, ck], k_ref[0, sb, ck], v_ref[0, sb, cv], lb_ref[0, sb, ck],
                                st_refs[2 * hh + 1], tri_b, mask_b, 0)
                o_ref[0, sb, cv] += ob
            return carry

        lax.fori_loop(0, n, body, 0)

    run(qc_ref, kc_ref, vc_ref, lfc_ref, lbc_ref, oc_ref)
    run(ql_ref, kl_ref, vl_ref, lfl_ref, lbl_ref, ol_ref)

    g = ng_ref[...]

    def norm(o_ref):
        rows = o_ref.shape[1]
        blk = _tile(rows, 256)

        def body(i, carry):
            s = pl.ds(pl.multiple_of(i * blk, blk), blk)
            for hh in range(nh):
                cv = slice(hh * dv, (hh + 1) * dv)
                o = o_ref[0, s, cv]
                o_ref[0, s, cv] = o * lax.rsqrt(jnp.mean(o * o, axis=-1, keepdims=True) + RMS_EPS) * g
            return carry

        lax.fori_loop(0, rows // blk, body, 0)

    norm(oc_ref)
    norm(ol_ref)


def _gla_scan(proj_c, proj_l, norm_g, b, lc, ll, hk, hv):
    dk, dv = hk // GLA_HEADS, hv // GLA_HEADS
    qc, kc, vc, _, lfc, lbc = [a.reshape(b, lc, -1) for a in proj_c]
    ql, kl, vl, _, lfl, lbl = [a.reshape(b, ll, -1) for a in proj_l]
    hp = 2
    head = lambda i, h: (i, 0, h)
    specs = []
    for n in (lc, ll):
        specs += [pl.BlockSpec((1, n, hp * dk), head), pl.BlockSpec((1, n, hp * dk), head),
                  pl.BlockSpec((1, n, hp * dv), head), pl.BlockSpec((1, n, hp * dk), head),
                  pl.BlockSpec((1, n, hp * dk), head)]
    specs.append(pl.BlockSpec((1, dv), lambda i, h: (0, 0)))
    oc, ol = pl.pallas_call(
        functools.partial(_gla_scan_kernel, dk=dk, dv=dv),
        grid=(b, GLA_HEADS // hp),
        in_specs=specs,
        out_specs=[pl.BlockSpec((1, lc, hp * dv), head), pl.BlockSpec((1, ll, hp * dv), head)],
        out_shape=[jax.ShapeDtypeStruct((b, lc, hv), F32), jax.ShapeDtypeStruct((b, ll, hv), F32)],
        scratch_shapes=[pltpu.VMEM((dv, dk), F32) for _ in range(2 * hp)],
        compiler_params=_cparams(("arbitrary", "arbitrary")),
        name="gla_scan",
    )(qc, kc, vc, lfc, lbc, ql, kl, vl, lfl, lbl, norm_g.reshape(1, dv))
    return oc, ol


def _mix_out_kernel(*refs, gated):
    if gated:
        o_ref, r_ref, h_ref, g_ref, w_ref, lng_ref, lnb_ref, out_ref = refs
        o = o_ref[...] * _silu(r_ref[...])
    else:
        o_ref, h_ref, g_ref, w_ref, lng_ref, lnb_ref, out_ref = refs
        o = o_ref[...]
    y = jnp.dot(o.astype(BF16), w_ref[...], preferred_element_type=F32)
    z = ALPHA * h_ref[...] + g_ref[0] * y
    out_ref[...] = _layer_norm(z, lng_ref[...], lnb_ref[...])


def _mix_out(o2, r2, h2, gate, mod_idx, w_out, ln_g, ln_b):
    t, d = h2.shape
    kdim = o2.shape[1]
    tm = _tile(t, 512)
    row = lambda i: (i, 0)
    const = lambda i: (0, 0)
    gated = r2 is not None
    ins = [o2] + ([r2] if gated else []) + [h2, gate, w_out, ln_g.reshape(1, d), ln_b.reshape(1, d)]
    specs = [pl.BlockSpec((tm, kdim), row)] + ([pl.BlockSpec((tm, kdim), row)] if gated else [])
    specs += [pl.BlockSpec((tm, d), row),
              pl.BlockSpec((1, 1, d), lambda i: (mod_idx(i, tm), 0, 0)),
              pl.BlockSpec(w_out.shape, const),
              pl.BlockSpec((1, d), const), pl.BlockSpec((1, d), const)]
    return pl.pallas_call(
        functools.partial(_mix_out_kernel, gated=gated),
        grid=(t // tm,),
        in_specs=specs,
        out_specs=pl.BlockSpec((tm, d), row),
        out_shape=jax.ShapeDtypeStruct((t, d), F32),
        compiler_params=_cparams(("arbitrary",)),
        name="mix_out",
    )(*ins)


def _oddeven_merge_sort_pairs(n):
    pairs = []
    p = 1
    while p < n:
        k = p
        while k >= 1:
            for j in range(k % p, n - k, 2 * k):
                for i in range(min(k, n - j - k)):
                    if (i + j) // (2 * p) == (i + j + k) // (2 * p):
                        pairs.append((i + j, i + j + k))
            k //= 2
        p *= 2
    return pairs


def _sublane_allmax(x):
    return jnp.broadcast_to(jnp.max(x, axis=0, keepdims=True), x.shape)


def _merge_top(rows, n):
    rows = list(rows)
    out = []
    for k in range(n):
        head = rows[0]
        m = _sublane_allmax(head)
        out.append(m)
        rem = n - 1 - k
        if rem == 0:
            break
        hit = head >= m
        for r in range(min(rem, len(rows))):
            nxt = rows[r + 1] if r + 1 < len(rows) else NEG_INF
            rows[r] = jnp.where(hit, nxt, rows[r])
    return out


def _peer_route(sa, sb):
    n = PEER_NEXT
    nr = PEER_NKEYS // SUBLANES

    def top(s):
        rows = [s[r * SUBLANES:(r + 1) * SUBLANES, :] for r in range(nr)]
        for i, j in _oddeven_merge_sort_pairs(nr):
            rows[i], rows[j] = jnp.maximum(rows[i], rows[j]), jnp.minimum(rows[i], rows[j])
        return _merge_top(rows, n)

    va = top(sa)
    vb = top(sb)
    sub = lax.broadcasted_iota(jnp.int32, (SUBLANES, LANES), 0)
    nj = 4
    lens = [n // (j + 1) for j in range(nj)]
    lens += [sum(1 for j in range(nj, n) if (i + 1) * (j + 1) <= n) for i in range(SUBLANES - nj)]
    assert sum(lens) == sum(n // (j + 1) for j in range(n)) and lens[-1] == 0
    len_t = jnp.zeros((SUBLANES, LANES), jnp.int32)
    vb_lo = vb[0]
    va_hi = va[0]
    for s in range(SUBLANES):
        len_t = jnp.where(sub == s, lens[s], len_t)
        if 0 < s < nj:
            vb_lo = jnp.where(sub == s, vb[s], vb_lo)
        if s > nj:
            va_hi = jnp.where(sub == s, va[s - nj], va_hi)
    cand = []
    for r in range(max(lens)):
        row = jnp.where(sub < nj, va[r] + vb_lo, va_hi + vb[min(nj + r, n - 1)])
        cand.append(jnp.where(len_t > r, row, NEG_INF))
    e = _merge_top(cand, n)
    tau = 0.5 * (e[PEER_TOPK - 1] + e[PEER_TOPK])
    z = jnp.ones_like(e[0])
    for k in range(1, PEER_TOPK):
        z = z + jnp.exp(e[k] - e[0])
    zinv = 1.0 / z
    tb = tau - vb[0]
    ea, eb, th = [], [], []
    for r in range(nr):
        sar = sa[r * SUBLANES:(r + 1) * SUBLANES, :]
        sbr = sb[r * SUBLANES:(r + 1) * SUBLANES, :]
        ea.append(jnp.exp(sar - va[0]) * (0.5 * zinv))
        eb.append(jnp.exp(sbr - vb[0]))
        th.append(jnp.exp(tb - sar))
    return (jnp.concatenate(ea, axis=0), jnp.concatenate(eb, axis=0), jnp.concatenate(th, axis=0))


def _peer_kernel(h_ref, sh_ref, sc_ref, gate_ref, wq_ref, ka_ref, kb_ref, u_ref, vt_ref,
                 lng_ref, lnb_ref, out_ref, xt_s, q_s, ea_s, eb_s, th_s, ht_s, p_s, yt_s):
    j = pl.program_id(1)
    tm = h_ref.shape[0]
    nlt = tm // LANES
    nk = PEER_NKEYS
    a_per = u_ref.shape[0] // nk

    @pl.when(j == 0)
    def _prologue():
        x = h_ref[...] * (1.0 + sc_ref[0]) + sh_ref[0]
        xt_s[...] = x.T.astype(BF16)

        def head_body(hd, carry):
            rq = pl.ds(pl.multiple_of(hd * 2 * nk, 2 * nk), 2 * nk)
            q_s[...] = jnp.dot(wq_ref[rq, :], xt_s[...], preferred_element_type=F32)
            sa = jnp.dot(ka_ref[hd], q_s[0:nk, :].astype(BF16), preferred_element_type=F32)
            sb = jnp.dot(kb_ref[hd], q_s[nk:, :].astype(BF16), preferred_element_type=F32)
            for lt in range(nlt):
                ls = slice(lt * LANES, (lt + 1) * LANES)
                ea, eb, th = _peer_route(sa[:, ls], sb[:, ls])
                ea_s[hd, :, ls] = ea
                eb_s[hd, :, ls] = eb
                th_s[hd, :, ls] = th
            return carry

        lax.fori_loop(0, PEER_HEADS, head_body, 0)
        yt_s[...] = jnp.zeros_like(yt_s)

    ht_s[...] = jnp.dot(u_ref[...], xt_s[...], preferred_element_type=F32)

    arows = pl.ds(pl.multiple_of(j * a_per, SUBLANES), a_per)

    def lane_body(lt, carry):
        ls = pl.ds(pl.multiple_of(lt * LANES, LANES), LANES)
        ths = [th_s[hd, arows, ls] for hd in range(PEER_HEADS)]
        eas = [ea_s[hd, arows, ls] for hd in range(PEER_HEADS)]
        for al in range(a_per):
            r = slice(al * nk, (al + 1) * nk)
            w = jnp.zeros((nk, LANES), F32)
            for hd in range(PEER_HEADS):
                ebv = eb_s[hd, :, ls]
                w = w + jnp.where(ebv >= ths[hd][al:al + 1, :], ebv, 0.0) * eas[hd][al:al + 1, :]
            hh = ht_s[r, ls]
            gl = hh * (1.0 + lax.erf(hh * (2.0 ** -0.5)))
            p_s[r, ls] = (w * gl).astype(BF16)
        return carry

    lax.fori_loop(0, nlt, lane_body, 0)
    yt_s[...] += jnp.dot(vt_ref[...], p_s[...], preferred_element_type=F32)

    @pl.when(j == pl.num_programs(1) - 1)
    def _epilogue():
        y = yt_s[...].T
        z = ALPHA * h_ref[...] + gate_ref[0] * y
        out_ref[...] = _layer_norm(z, lng_ref[...], lnb_ref[...])


def _peer(h2, sh, sc, gate, mod_idx, wq_t, ka, kb, u_b, vt_b, ln_g, ln_b):
    t, d = h2.shape
    ne = u_b.shape[0]
    tm = _tile(t, 512)
    ce = 2 * SUBLANES * PEER_NKEYS
    row = lambda i, j: (i, 0)
    const2 = lambda i, j: (0, 0)
    const3 = lambda i, j: (0, 0, 0)
    mod = lambda i, j: (mod_idx(i, tm), 0, 0)
    factors = (PEER_HEADS, PEER_NKEYS, tm)
    return pl.pallas_call(
        _peer_kernel,
        grid=(t // tm, ne // ce),
        in_specs=[pl.BlockSpec((tm, d), row),
                  pl.BlockSpec((1, 1, d), mod), pl.BlockSpec((1, 1, d), mod),
                  pl.BlockSpec((1, 1, d), mod),
                  pl.BlockSpec(wq_t.shape, const2),
                  pl.BlockSpec(ka.shape, const3), pl.BlockSpec(kb.shape, const3),
                  pl.BlockSpec((ce, d), lambda i, j: (j, 0)),
                  pl.BlockSpec((d, ce), lambda i, j: (0, j)),
                  pl.BlockSpec((1, d), const2), pl.BlockSpec((1, d), const2)],
        out_specs=pl.BlockSpec((tm, d), row),
        out_shape=jax.ShapeDtypeStruct((t, d), F32),
        scratch_shapes=[pltpu.VMEM((d, tm), BF16),
                        pltpu.VMEM((2 * PEER_NKEYS, tm), F32),
                        pltpu.VMEM(factors, F32), pltpu.VMEM(factors, F32), pltpu.VMEM(factors, F32),
                        pltpu.VMEM((ce, tm), F32),
                        pltpu.VMEM((ce, tm), BF16),
                        pltpu.VMEM((d, tm), F32)],
        compiler_params=_cparams(("arbitrary", "arbitrary")),
        name="peer",
    )(h2, sh, sc, gate, wq_t, ka, kb, u_b, vt_b, ln_g.reshape(1, d), ln_b.reshape(1, d))


def _mla_proj_kernel(*refs, rotate, want_q):
    if rotate:
        (h_ref, sh_ref, sc_ref, wd_ref, qg_ref, kvg_ref, wuq_ref, wukv_ref,
         cq_ref, sq_ref, ck_ref, sk_ref) = refs[:12]
        outs = refs[12:]
    else:
        h_ref, sh_ref, sc_ref, wd_ref, qg_ref, kvg_ref, wuq_ref, wukv_ref = refs[:8]
        outs = refs[8:]
    if want_q:
        q_ref, k_ref, v_ref = outs
    else:
        k_ref, v_ref = outs
    nh = MLA_HEADS
    x = h_ref[0] * (1.0 + sc_ref[0]) + sh_ref[0]
    dn = jnp.dot(x.astype(BF16), wd_ref[...], preferred_element_type=F32)
    cq = dn[:, :MLA_Q_RANK]
    ckv = dn[:, MLA_Q_RANK:MLA_Q_RANK + MLA_KV_RANK]
    o = MLA_Q_RANK + MLA_KV_RANK
    kr = dn[:, o:o + MLA_ROPE]
    kr_sw = dn[:, o + MLA_ROPE:o + 2 * MLA_ROPE]
    if rotate:
        kr = kr * ck_ref[...] + kr_sw * sk_ref[...]
    ckvn = ckv * lax.rsqrt(jnp.mean(ckv * ckv, axis=-1, keepdims=True) + RMS_EPS) * kvg_ref[...]
    kv = jnp.dot(ckvn.astype(BF16), wukv_ref[...], preferred_element_type=F32)
    tm = x.shape[0]
    zpad = jnp.zeros((tm, MLA_QK_PAD - MLA_NOPE - MLA_ROPE), F32)
    krp = jnp.concatenate([kr, zpad], axis=-1).astype(BF16)
    for hd in range(nh):
        k_ref[0, hd, :, 0:MLA_NOPE] = kv[:, hd * MLA_NOPE:(hd + 1) * MLA_NOPE].astype(BF16)
        k_ref[0, hd, :, MLA_NOPE:] = krp
        v_ref[0, hd] = kv[:, nh * MLA_NOPE + hd * MLA_V:nh * MLA_NOPE + (hd + 1) * MLA_V].astype(BF16)
    if want_q:
        cqn = cq * lax.rsqrt(jnp.mean(cq * cq, axis=-1, keepdims=True) + RMS_EPS) * qg_ref[...]
        q = jnp.dot(cqn.astype(BF16), wuq_ref[...], preferred_element_type=F32)
        scale = (MLA_NOPE + MLA_ROPE) ** -0.5
        base = nh * MLA_NOPE
        qr = q[:, base:base + nh * MLA_ROPE]
        if rotate:
            qr_sw = q[:, base + nh * MLA_ROPE:]
            qr = qr * cq_ref[...] + qr_sw * sq_ref[...]
        for hd in range(nh):
            q_ref[0, hd, :, 0:MLA_NOPE] = (q[:, hd * MLA_NOPE:(hd + 1) * MLA_NOPE] * scale).astype(BF16)
            qrp = jnp.concatenate([qr[:, hd * MLA_ROPE:(hd + 1) * MLA_ROPE] * scale, zpad], axis=-1)
            q_ref[0, hd, :, MLA_NOPE:] = qrp.astype(BF16)


def _mla_proj(h3, sh, sc, mod_idx, wd, qg, kvg, wuq, wukv, rope, want_q):
    b, n, d = h3.shape
    tm = _tile(n, 256)
    nt = n // tm
    rotate = rope is not None
    tok = lambda i, t: (i, t, 0)
    const = lambda i, t: (0, 0)
    mod = lambda i, t: (mod_idx(i), 0, 0)
    ins = [h3, sh, sc, wd, qg, kvg, wuq, wukv]
    specs = [pl.BlockSpec((1, tm, d), tok), pl.BlockSpec((1, 1, d), mod), pl.BlockSpec((1, 1, d), mod),
             pl.BlockSpec(wd.shape, const), pl.BlockSpec(qg.shape, const), pl.BlockSpec(kvg.shape, const),
             pl.BlockSpec(wuq.shape, const), pl.BlockSpec(wukv.shape, const)]
    if rotate:
        ins += list(rope)
        specs += [pl.BlockSpec((tm, a.shape[1]), lambda i, t: (t, 0)) for a in rope]
    hd4 = lambda i, t: (i, 0, t, 0)
    out_specs, out_shape = [], []
    if want_q:
        out_specs.append(pl.BlockSpec((1, MLA_HEADS, tm, MLA_QK_PAD), hd4))
        out_shape.append(jax.ShapeDtypeStruct((b, MLA_HEADS, n, MLA_QK_PAD), BF16))
    out_specs += [pl.BlockSpec((1, MLA_HEADS, tm, MLA_QK_PAD), hd4),
                  pl.BlockSpec((1, MLA_HEADS, tm, MLA_V), hd4)]
    out_shape += [jax.ShapeDtypeStruct((b, MLA_HEADS, n, MLA_QK_PAD), BF16),
                  jax.ShapeDtypeStruct((b, MLA_HEADS, n, MLA_V), BF16)]

    return pl.pallas_call(
        functools.partial(_mla_proj_kernel, rotate=rotate, want_q=want_q),
        grid=(b, nt),
        in_specs=specs,
        out_specs=out_specs,
        out_shape=out_shape,
        compiler_params=_cparams(("arbitrary", "arbitrary")),
        name="mla_proj_q" if want_q else "mla_proj_kv",
    )(*ins)


def _mla_attn_kernel(q_ref, kc_ref, kl_ref, vc_ref, vl_ref, o_ref):
    nt = (((1,), (1,)), ((), ()))
    for hh in range(q_ref.shape[1]):
        q = q_ref[0, hh]
        s_c = lax.dot_general(q, kc_ref[0, hh], nt, preferred_element_type=F32)
        s_l = lax.dot_general(q, kl_ref[0, hh], nt, preferred_element_type=F32)
        m = jnp.maximum(jnp.max(s_c, axis=-1, keepdims=True), jnp.max(s_l, axis=-1, keepdims=True))
        p_c = jnp.exp(s_c - m)
        p_l = jnp.exp(s_l - m)
        den = jnp.sum(p_c, axis=-1, keepdims=True) + jnp.sum(p_l, axis=-1, keepdims=True)
        o = (jnp.dot(p_c.astype(BF16), vc_ref[0, hh], preferred_element_type=F32)
             + jnp.dot(p_l.astype(BF16), vl_ref[0, hh], preferred_element_type=F32))
        o_ref[0, :, hh * MLA_V:(hh + 1) * MLA_V] = (o / den).astype(o_ref.dtype)


def _mla_attn(q, kc, kl, vc, vl):
    b, nh, n, dq = q.shape
    lc = kc.shape[2]
    tq = _tile(n, 256)
    hp = 8
    kv = lambda i, h, t: (i, h, 0, 0)
    return pl.pallas_call(
        _mla_attn_kernel,
        grid=(b, nh // hp, n // tq),
        in_specs=[pl.BlockSpec((1, hp, tq, dq), lambda i, h, t: (i, h, t, 0)),
                  pl.BlockSpec((1, hp, lc, dq), kv), pl.BlockSpec((1, hp, n, dq), kv),
                  pl.BlockSpec((1, hp, lc, MLA_V), kv), pl.BlockSpec((1, hp, n, MLA_V), kv)],
        out_specs=pl.BlockSpec((1, tq, hp * MLA_V), lambda i, h, t: (i, t, h)),
        out_shape=jax.ShapeDtypeStruct((b, n, nh * MLA_V), BF16),
        compiler_params=_cparams(("arbitrary", "arbitrary", "arbitrary")),
        name="mla_attn",
    )(q, kc, kl, vc, vl)


def _rope_tables(n_tokens, reps):
    rows = n_tokens // GRID_W
    row = jnp.repeat(jnp.arange(rows, dtype=F32), GRID_W)
    col = jnp.tile(jnp.arange(GRID_W, dtype=F32), rows)
    n_freq = MLA_ROPE // 4
    inv_freq = ROPE_BASE ** (-jnp.arange(n_freq, dtype=F32) / n_freq)
    ar = row[:, None] * inv_freq
    ac = col[:, None] * inv_freq
    cos = jnp.concatenate([jnp.cos(ar), jnp.cos(ar), jnp.cos(ac), jnp.cos(ac)], axis=-1)
    sin = jnp.concatenate([-jnp.sin(ar), jnp.sin(ar), -jnp.sin(ac), jnp.sin(ac)], axis=-1)
    return jnp.tile(cos, (1, reps)), jnp.tile(sin, (1, reps))


def _swap_halves_cols(w):
    q = MLA_ROPE // 4
    return jnp.concatenate([w[..., q:2 * q], w[..., 0:q], w[..., 3 * q:4 * q], w[..., 2 * q:3 * q]], axis=-1)


def kernel(x, c, ctx, c_ctx, ada_w, ada_b, ln_tm_g, ln_tm_b, ln_cm_g, ln_cm_b, gla_w_in, gla_gate_fwd_a, gla_gate_fwd_b, gla_gate_fwd_bias, gla_gate_bwd_a, gla_gate_bwd_b, gla_gate_bwd_bias, gla_norm_g, gla_w_out, mla_w_down, mla_q_norm_g, mla_kv_norm_g, mla_w_uq, mla_w_ukv, mla_w_out, peer_w_query, peer_keys_a, peer_keys_b, peer_u, peer_v):
    b, ll, d = x.shape
    lc = ctx.shape[1]
    tl, tc = b * ll, b * lc
    assert DEPTH == ada_w.shape[0] == 2

    nb = -(-(b + 1) // SUBLANES) * SUBLANES
    c_all = jnp.concatenate([c, c_ctx[None, :], jnp.zeros((nb - b - 1, d), F32)], axis=0)
    mod = _ada(c_all, ada_w, ada_b).reshape(DEPTH, nb, 6, 1, d)

    def lat_idx(tokens_per_row):
        return lambda i, tm: (i * tm) // tokens_per_row

    lat_mod = lat_idx(ll)
    ctx_mod = lambda i, tm: b

    h_lat = x.reshape(tl, d)
    h_ctx = ctx.reshape(tc, d)

    def peer_weights(i):
        wq_t = peer_w_query[i].T.astype(BF16)
        return (wq_t, peer_keys_a[i].astype(BF16), peer_keys_b[i].astype(BF16),
                peer_u[i].astype(BF16), peer_v[i].T.astype(BF16))

    m0 = [mod[0, :, k] for k in range(6)]
    hk = gla_gate_fwd_b.shape[2]
    hv = (gla_w_in.shape[2] - 2 * hk) // 2
    w_in = gla_w_in[0].astype(BF16)
    ga = jnp.concatenate([gla_gate_fwd_a[0], gla_gate_bwd_a[0]], axis=1).astype(BF16)
    zr = jnp.zeros((GLA_GATE_RANK, hk), F32)
    gb = jnp.concatenate([jnp.concatenate([gla_gate_fwd_b[0], zr], axis=1),
                          jnp.concatenate([zr, gla_gate_bwd_b[0]], axis=1)], axis=0).astype(BF16)
    gbias = jnp.concatenate([gla_gate_fwd_bias[0], gla_gate_bwd_bias[0]])[None, :]
    proj_l = _gla_inproj(h_lat, m0[0], m0[1], lat_mod, w_in, ga, gb, gbias, hk, hv)
    proj_c = _gla_inproj(h_ctx, m0[0], m0[1], ctx_mod, w_in, ga, gb, gbias, hk, hv)
    on_c, on_l = _gla_scan(proj_c, proj_l, gla_norm_g[0], b, lc, ll, hk, hv)
    w_out = gla_w_out[0].astype(BF16)
    h_lat = _mix_out(on_l.reshape(tl, hv), proj_l[3], h_lat, m0[2], lat_mod, w_out, ln_tm_g[0], ln_tm_b[0])
    h_ctx = _mix_out(on_c.reshape(tc, hv), proj_c[3], h_ctx, m0[2], ctx_mod, w_out, ln_tm_g[0], ln_tm_b[0])
    pw = peer_weights(0)
    h_lat = _peer(h_lat, m0[3], m0[4], m0[5], lat_mod, *pw, ln_cm_g[0], ln_cm_b[0])
    h_ctx = _peer(h_ctx, m0[3], m0[4], m0[5], ctx_mod, *pw, ln_cm_g[0], ln_cm_b[0])

    m1 = [mod[1, :, k] for k in range(6)]
    nh = MLA_HEADS
    wd = mla_w_down[0]
    o = MLA_Q_RANK + MLA_KV_RANK
    wd = jnp.concatenate([wd, _swap_halves_cols(wd[:, o:])], axis=1).astype(BF16)
    wuq = mla_w_uq[0].reshape(MLA_Q_RANK, nh, MLA_NOPE + MLA_ROPE)
    wuq_rope = wuq[:, :, MLA_NOPE:]
    wuq = jnp.concatenate([wuq[:, :, :MLA_NOPE].reshape(MLA_Q_RANK, -1),
                           wuq_rope.reshape(MLA_Q_RANK, -1),
                           _swap_halves_cols(wuq_rope).reshape(MLA_Q_RANK, -1)], axis=1).astype(BF16)
    wukv = mla_w_ukv[0].reshape(MLA_KV_RANK, nh, MLA_NOPE + MLA_V)
    wukv = jnp.concatenate([wukv[:, :, :MLA_NOPE].reshape(MLA_KV_RANK, -1),
                            wukv[:, :, MLA_NOPE:].reshape(MLA_KV_RANK, -1)], axis=1).astype(BF16)
    qg = mla_q_norm_g[0][None, :]
    kvg = mla_kv_norm_g[0][None, :]
    cos_q, sin_q = _rope_tables(ll, nh)
    cos_k, sin_k = _rope_tables(ll, 1)
    q_l, k_l, v_l = _mla_proj(h_lat.reshape(b, ll, d), m1[0], m1[1], lambda i: i, wd, qg, kvg, wuq, wukv,
                              (cos_q, sin_q, cos_k, sin_k), True)
    k_c, v_c = _mla_proj(h_ctx.reshape(b, lc, d), m1[0], m1[1], lambda i: b, wd, qg, kvg, wuq, wukv,
                         None, False)
    o_l = _mla_attn(q_l, k_c, k_l, v_c, v_l)
    h_lat = _mix_out(o_l.reshape(tl, nh * MLA_V), None, h_lat, m1[2], lat_mod, mla_w_out[0].astype(BF16),
                     ln_tm_g[1], ln_tm_b[1])
    pw = peer_weights(1)
    h_lat = _peer(h_lat, m1[3], m1[4], m1[5], lat_mod, *pw, ln_cm_g[1], ln_cm_b[1])
    return h_lat.reshape(b, ll, d)
```

```python
import functools
import math

import jax
import jax.numpy as jnp
from jax import lax
from jax.experimental import pallas as pl
from jax.experimental.pallas import tpu as pltpu

F32 = jnp.float32
BF16 = jnp.bfloat16
NEG_INF = float("-inf")

DEPTH = 2
ALPHA = (2.0 * DEPTH) ** 0.25
LN_EPS = 1e-5
RMS_EPS = 1e-6

GLA_HEADS = 4
GLA_TAU = 16.0
GLA_CHUNK = 128
GLA_GATE_RANK = 16

MLA_HEADS = 8
MLA_NOPE = 128
MLA_ROPE = 64
MLA_V = 128
MLA_Q_RANK = 256
MLA_KV_RANK = 128
MLA_QK_PAD = 256
GRID_W = 64
ROPE_BASE = 10000.0

PEER_HEADS = 8
PEER_NKEYS = 128
PEER_TOPK = 16
PEER_NEXT = PEER_TOPK + 1

LANES = 128
SUBLANES = 8
VMEM_LIMIT = 56 * 1024 * 1024


def _cparams(sem, flags=None):
    return pltpu.CompilerParams(dimension_semantics=sem, vmem_limit_bytes=VMEM_LIMIT, flags=flags)


def _layer_norm(z, g, b):
    mu = jnp.mean(z, axis=-1, keepdims=True)
    zc = z - mu
    var = jnp.mean(zc * zc, axis=-1, keepdims=True)
    return zc * lax.rsqrt(var + LN_EPS) * g + b


def _silu(x):
    return x / (1.0 + jnp.exp(-x))


def _tile(n, pref):
    t = pref
    while n % t:
        t //= 2
    return t


def _ada_kernel(c_ref, w_ref, b_ref, o_ref):
    s = _silu(c_ref[...])
    o_ref[0] = jnp.dot(s, w_ref[0], preferred_element_type=F32,
                       precision=lax.Precision.HIGHEST) + b_ref[0]


def _ada(c_all, ada_w, ada_b):
    nb, d = c_all.shape
    depth, _, n6 = ada_w.shape
    tn = 1024
    return pl.pallas_call(
        _ada_kernel,
        grid=(depth, n6 // tn),
        in_specs=[pl.BlockSpec((nb, d), lambda i, j: (0, 0)),
                  pl.BlockSpec((1, d, tn), lambda i, j: (i, 0, j)),
                  pl.BlockSpec((1, 1, tn), lambda i, j: (i, 0, j))],
        out_specs=pl.BlockSpec((1, nb, tn), lambda i, j: (i, 0, j)),
        out_shape=jax.ShapeDtypeStruct((depth, nb, n6), F32),
        compiler_params=_cparams(("arbitrary", "arbitrary")),
        name="ada_mod",
    )(c_all, ada_w, ada_b.reshape(depth, 1, n6))


def _gla_inproj_kernel(h_ref, sh_ref, sc_ref, w_ref, ga_ref, gb_ref, gbias_ref,
                       q_ref, k_ref, v_ref, r_ref, lf_ref, lb_ref, *, hk, hv, dk):
    x = h_ref[...] * (1.0 + sc_ref[0]) + sh_ref[0]
    xb = x.astype(BF16)
    y = jnp.dot(xb, w_ref[...], preferred_element_type=F32)
    q_ref[...] = y[:, :hk] * (dk ** -0.5)
    k_ref[...] = y[:, hk:2 * hk]
    v_ref[...] = y[:, 2 * hk:2 * hk + hv]
    r_ref[...] = y[:, 2 * hk + hv:]
    g1 = jnp.dot(xb, ga_ref[...], preferred_element_type=F32)
    z = jnp.dot(g1.astype(BF16), gb_ref[...], preferred_element_type=F32) + gbias_ref[...]
    ls = (jnp.minimum(z, 0.0) - jnp.log1p(jnp.exp(-jnp.abs(z)))) * (1.0 / GLA_TAU)
    lf_ref[...] = ls[:, :hk]
    lb_ref[...] = ls[:, hk:]


def _gla_inproj(h2, sh, sc, mod_idx, w_in, ga, gb, gbias, hk, hv):
    t, d = h2.shape
    tm = _tile(t, 256)
    row = lambda i: (i, 0)
    const = lambda i: (0, 0)
    kern = functools.partial(_gla_inproj_kernel, hk=hk, hv=hv, dk=hk // GLA_HEADS)
    return pl.pallas_call(
        kern,
        grid=(t // tm,),
        in_specs=[pl.BlockSpec((tm, d), row),
                  pl.BlockSpec((1, 1, d), lambda i: (mod_idx(i, tm), 0, 0)),
                  pl.BlockSpec((1, 1, d), lambda i: (mod_idx(i, tm), 0, 0)),
                  pl.BlockSpec(w_in.shape, const),
                  pl.BlockSpec(ga.shape, const),
                  pl.BlockSpec(gb.shape, const),
                  pl.BlockSpec(gbias.shape, const)],
        out_specs=[pl.BlockSpec((tm, hk), row), pl.BlockSpec((tm, hk), row),
                   pl.BlockSpec((tm, hv), row), pl.BlockSpec((tm, hv), row),
                   pl.BlockSpec((tm, hk), row), pl.BlockSpec((tm, hk), row)],
        out_shape=[jax.ShapeDtypeStruct((t, hk), F32), jax.ShapeDtypeStruct((t, hk), F32),
                   jax.ShapeDtypeStruct((t, hv), F32), jax.ShapeDtypeStruct((t, hv), F32),
                   jax.ShapeDtypeStruct((t, hk), F32), jax.ShapeDtypeStruct((t, hk), F32)],
        compiler_params=_cparams(("arbitrary",)),
        name="gla_inproj",
    )(h2, sh, sc, w_in, ga, gb, gbias)


def _split3(x):
    x1 = x.astype(BF16)
    r1 = x - x1.astype(F32)
    x2 = r1.astype(BF16)
    x3 = (r1 - x2.astype(F32)).astype(BF16)
    return x1, x2, x3


def _gla_chunk(q, k, v, lg, st_ref, tri, mask, last_row):
    l1, l2, l3 = _split3(lg)
    cum = (jnp.dot(tri, l1, preferred_element_type=F32)
           + jnp.dot(tri, l2, preferred_element_type=F32)
           + jnp.dot(tri, l3, preferred_element_type=F32))
    last = cum[last_row:last_row + 1, :]
    qe = (q * jnp.exp(cum)).astype(BF16)
    ke = (k * jnp.exp(-cum)).astype(BF16)
    kd = (k * jnp.exp(last - cum)).astype(BF16)
    vb = v.astype(BF16)
    a = lax.dot_general(qe, ke, (((1,), (1,)), ((), ())), preferred_element_type=F32)
    a = jnp.where(mask, a, 0.0)
    st = st_ref[...]
    o = jnp.dot(a.astype(BF16), vb, preferred_element_type=F32)
    o = o + lax.dot_general(qe, st.astype(BF16), (((1,), (1,)), ((), ())),
                            preferred_element_type=F32)
    kvt = lax.dot_general(vb, kd, (((0,), (0,)), ((), ())), preferred_element_type=F32)
    st_ref[...] = st * jnp.exp(last) + kvt
    return o


def _gla_scan_kernel(qc_ref, kc_ref, vc_ref, lfc_ref, lbc_ref,
                     ql_ref, kl_ref, vl_ref, lfl_ref, lbl_ref, ng_ref,
                     oc_ref, ol_ref, *st_refs, dk, dv):
    c = GLA_CHUNK
    nh = len(st_refs) // 2
    ii = lax.broadcasted_iota(jnp.int32, (c, c), 0)
    jj = lax.broadcasted_iota(jnp.int32, (c, c), 1)
    mask_f = ii >= jj
    mask_b = ii < jj
    tri_f = jnp.where(mask_f, 1.0, 0.0).astype(BF16)
    tri_b = jnp.where(ii <= jj, 1.0, 0.0).astype(BF16)

    for st in st_refs:
        st[...] = jnp.zeros_like(st)
    oc_ref[...] = jnp.zeros_like(oc_ref)
    ol_ref[...] = jnp.zeros_like(ol_ref)

    def run(q_ref, k_ref, v_ref, lf_ref, lb_ref, o_ref):
        n = q_ref.shape[1] // c

        def body(i, carry):
            sf = pl.ds(pl.multiple_of(i * c, c), c)
            sb = pl.ds(pl.multiple_of((n - 1 - i) * c, c), c)
            for hh in range(nh):
                ck = slice(hh * dk, (hh + 1) * dk)
                cv = slice(hh * dv, (hh + 1) * dv)
                of = _gla_chunk(q_ref[0, sf, ck], k_ref[0, sf, ck], v_ref[0, sf, cv], lf_ref[0, sf, ck],
                                st_refs[2 * hh], tri_f, mask_f, c - 1)
                o_ref[0, sf, cv] += of
                ob = _gla_chunk(q_ref[0, sb, ck], k_ref[0, sb, ck], v_ref[0, sb, cv], lb_ref[0, sb, ck],
                                st_refs[2 * hh + 1], tri_b, mask_b, 0)
                o_ref[0, sb, cv] += ob
            return carry

        lax.fori_loop(0, n, body, 0)

    run(qc_ref, kc_ref, vc_ref, lfc_ref, lbc_ref, oc_ref)
    run(ql_ref, kl_ref, vl_ref, lfl_ref, lbl_ref, ol_ref)

    g = ng_ref[...]

    def norm(o_ref):
        rows = o_ref.shape[1]
        blk = _tile(rows, 256)

        def body(i, carry):
            s = pl.ds(pl.multiple_of(i * blk, blk), blk)
            for hh in range(nh):
                cv = slice(hh * dv, (hh + 1) * dv)
                o = o_ref[0, s, cv]
                o_ref[0, s, cv] = o * lax.rsqrt(jnp.mean(o * o, axis=-1, keepdims=True) + RMS_EPS) * g
            return carry

        lax.fori_loop(0, rows // blk, body, 0)

    norm(oc_ref)
    norm(ol_ref)


def _gla_scan(proj_c, proj_l, norm_g, b, lc, ll, hk, hv):
    dk, dv = hk // GLA_HEADS, hv // GLA_HEADS
    qc, kc, vc, _, lfc, lbc = [a.reshape(b, lc, -1) for a in proj_c]
    ql, kl, vl, _, lfl, lbl = [a.reshape(b, ll, -1) for a in proj_l]
    hp = 2
    head = lambda i, h: (i, 0, h)
    specs = []
    for n in (lc, ll):
        specs += [pl.BlockSpec((1, n, hp * dk), head), pl.BlockSpec((1, n, hp * dk), head),
                  pl.BlockSpec((1, n, hp * dv), head), pl.BlockSpec((1, n, hp * dk), head),
                  pl.BlockSpec((1, n, hp * dk), head)]
    specs.append(pl.BlockSpec((1, dv), lambda i, h: (0, 0)))
    oc, ol = pl.pallas_call(
        functools.partial(_gla_scan_kernel, dk=dk, dv=dv),
        grid=(b, GLA_HEADS // hp),
        in_specs=specs,
        out_specs=[pl.BlockSpec((1, lc, hp * dv), head), pl.BlockSpec((1, ll, hp * dv), head)],
        out_shape=[jax.ShapeDtypeStruct((b, lc, hv), F32), jax.ShapeDtypeStruct((b, ll, hv), F32)],
        scratch_shapes=[pltpu.VMEM((dv, dk), F32) for _ in range(2 * hp)],
        compiler_params=_cparams(("arbitrary", "arbitrary")),
        name="gla_scan",
    )(qc, kc, vc, lfc, lbc, ql, kl, vl, lfl, lbl, norm_g.reshape(1, dv))
    return oc, ol


def _mix_out_kernel(*refs, gated):
    if gated:
        o_ref, r_ref, h_ref, g_ref, w_ref, lng_ref, lnb_ref, out_ref = refs
        o = o_ref[...] * _silu(r_ref[...])
    else:
        o_ref, h_ref, g_ref, w_ref, lng_ref, lnb_ref, out_ref = refs
        o = o_ref[...]
    y = jnp.dot(o.astype(BF16), w_ref[...], preferred_element_type=F32)
    z = ALPHA * h_ref[...] + g_ref[0] * y
    out_ref[...] = _layer_norm(z, lng_ref[...], lnb_ref[...])


def _mix_out(o2, r2, h2, gate, mod_idx, w_out, ln_g, ln_b):
    t, d = h2.shape
    kdim = o2.shape[1]
    tm = _tile(t, 512)
    row = lambda i: (i, 0)
    const = lambda i: (0, 0)
    gated = r2 is not None
    ins = [o2] + ([r2] if gated else []) + [h2, gate, w_out, ln_g.reshape(1, d), ln_b.reshape(1, d)]
    specs = [pl.BlockSpec((tm, kdim), row)] + ([pl.BlockSpec((tm, kdim), row)] if gated else [])
    specs += [pl.BlockSpec((tm, d), row),
              pl.BlockSpec((1, 1, d), lambda i: (mod_idx(i, tm), 0, 0)),
              pl.BlockSpec(w_out.shape, const),
              pl.BlockSpec((1, d), const), pl.BlockSpec((1, d), const)]
    return pl.pallas_call(
        functools.partial(_mix_out_kernel, gated=gated),
        grid=(t // tm,),
        in_specs=specs,
        out_specs=pl.BlockSpec((tm, d), row),
        out_shape=jax.ShapeDtypeStruct((t, d), F32),
        compiler_params=_cparams(("arbitrary",)),
        name="mix_out",
    )(*ins)


def _oddeven_merge_sort_pairs(n):
    pairs = []
    p = 1
    while p < n:
        k = p
        while k >= 1:
            for j in range(k % p, n - k, 2 * k):
                for i in range(min(k, n - j - k)):
                    if (i + j) // (2 * p) == (i + j + k) // (2 * p):
                        pairs.append((i + j, i + j + k))
            k //= 2
        p *= 2
    return pairs


def _sublane_allmax(x):
    return jnp.broadcast_to(jnp.max(x, axis=0, keepdims=True), x.shape)


def _merge_top(rows, n):
    rows = list(rows)
    out = []
    for k in range(n):
        head = rows[0]
        m = _sublane_allmax(head)
        out.append(m)
        rem = n - 1 - k
        if rem == 0:
            break
        hit = head >= m
        for r in range(min(rem, len(rows))):
            nxt = rows[r + 1] if r + 1 < len(rows) else NEG_INF
            rows[r] = jnp.where(hit, nxt, rows[r])
    return out


def _peer_route(sa, sb):
    n = PEER_NEXT
    nr = PEER_NKEYS // SUBLANES

    def top(s):
        rows = [s[r * SUBLANES:(r + 1) * SUBLANES, :] for r in range(nr)]
        for i, j in _oddeven_merge_sort_pairs(nr):
            rows[i], rows[j] = jnp.maximum(rows[i], rows[j]), jnp.minimum(rows[i], rows[j])
        return _merge_top(rows, n)

    va = top(sa)
    vb = top(sb)
    sub = lax.broadcasted_iota(jnp.int32, (SUBLANES, LANES), 0)
    nj = 4
    lens = [n // (j + 1) for j in range(nj)]
    lens += [sum(1 for j in range(nj, n) if (i + 1) * (j + 1) <= n) for i in range(SUBLANES - nj)]
    assert sum(lens) == sum(n // (j + 1) for j in range(n)) and lens[-1] == 0
    len_t = jnp.zeros((SUBLANES, LANES), jnp.int32)
    vb_lo = vb[0]
    va_hi = va[0]
    for s in range(SUBLANES):
        len_t = jnp.where(sub == s, lens[s], len_t)
        if 0 < s < nj:
            vb_lo = jnp.where(sub == s, vb[s], vb_lo)
        if s > nj:
            va_hi = jnp.where(sub == s, va[s - nj], va_hi)
    cand = []
    for r in range(max(lens)):
        row = jnp.where(sub < nj, va[r] + vb_lo, va_hi + vb[min(nj + r, n - 1)])
        cand.append(jnp.where(len_t > r, row, NEG_INF))
    e = _merge_top(cand, n)
    tau = 0.5 * (e[PEER_TOPK - 1] + e[PEER_TOPK])
    z = jnp.ones_like(e[0])
    for k in range(1, PEER_TOPK):
        z = z + jnp.exp(e[k] - e[0])
    zinv = 1.0 / z
    tb = tau - vb[0]
    ea, eb, th = [], [], []
    for r in range(nr):
        sar = sa[r * SUBLANES:(r + 1) * SUBLANES, :]
        sbr = sb[r * SUBLANES:(r + 1) * SUBLANES, :]
        ea.append(jnp.exp(sar - va[0]) * (0.5 * zinv))
        eb.append(jnp.exp(sbr - vb[0]))
        th.append(jnp.exp(tb - sar))
    return (jnp.concatenate(ea, axis=0), jnp.concatenate(eb, axis=0), jnp.concatenate(th, axis=0))


def _peer_kernel(h_ref, sh_ref, sc_ref, gate_ref, wq_ref, ka_ref, kb_ref, u_ref, vt_ref,
                 lng_ref, lnb_ref, out_ref, xt_s, q_s, ea_s, eb_s, th_s, ht_s, p_s, yt_s):
    j = pl.program_id(1)
    tm = h_ref.shape[0]
    nlt = tm // LANES
    nk = PEER_NKEYS
    a_per = u_ref.shape[0] // nk

    @pl.when(j == 0)
    def _prologue():
        x = h_ref[...] * (1.0 + sc_ref[0]) + sh_ref[0]
        xt_s[...] = x.T.astype(BF16)

        def head_body(hd, carry):
            rq = pl.ds(pl.multiple_of(hd * 2 * nk, 2 * nk), 2 * nk)
            q_s[...] = jnp.dot(wq_ref[rq, :], xt_s[...], preferred_element_type=F32)
            sa = jnp.dot(ka_ref[hd], q_s[0:nk, :].astype(BF16), preferred_element_type=F32)
            sb = jnp.dot(kb_ref[hd], q_s[nk:, :].astype(BF16), preferred_element_type=F32)
            for lt in range(nlt):
                ls = slice(lt * LANES, (lt + 1) * LANES)
                ea, eb, th = _peer_route(sa[:, ls], sb[:, ls])
                ea_s[hd, :, ls] = ea
                eb_s[hd, :, ls] = eb
                th_s[hd, :, ls] = th
            return carry

        lax.fori_loop(0, PEER_HEADS, head_body, 0)
        yt_s[...] = jnp.zeros_like(yt_s)

    ht_s[...] = jnp.dot(u_ref[...], xt_s[...], preferred_element_type=F32)

    arows = pl.ds(pl.multiple_of(j * a_per, SUBLANES), a_per)

    def lane_body(lt, carry):
        ls = pl.ds(pl.multiple_of(lt * LANES, LANES), LANES)
        ths = [th_s[hd, arows, ls] for hd in range(PEER_HEADS)]
        eas = [ea_s[hd, arows, ls] for hd in range(PEER_HEADS)]
        for al in range(a_per):
            r = slice(al * nk, (al + 1) * nk)
            w = jnp.zeros((nk, LANES), F32)
            for hd in range(PEER_HEADS):
                ebv = eb_s[hd, :, ls]
                w = w + jnp.where(ebv >= ths[hd][al:al + 1, :], ebv, 0.0) * eas[hd][al:al + 1, :]
            hh = ht_s[r, ls]
            gl = hh * (1.0 + lax.erf(hh * (2.0 ** -0.5)))
            p_s[r, ls] = (w * gl).astype(BF16)
        return carry

    lax.fori_loop(0, nlt, lane_body, 0)
    yt_s[...] += jnp.dot(vt_ref[...], p_s[...], preferred_element_type=F32)

    @pl.when(j == pl.num_programs(1) - 1)
    def _epilogue():
        y = yt_s[...].T
        z = ALPHA * h_ref[...] + gate_ref[0] * y
        out_ref[...] = _layer_norm(z, lng_ref[...], lnb_ref[...])


def _peer(h2, sh, sc, gate, mod_idx, wq_t, ka, kb, u_b, vt_b, ln_g, ln_b):
    t, d = h2.shape
    ne = u_b.shape[0]
    tm = _tile(t, 512)
    ce = 2 * SUBLANES * PEER_NKEYS
    row = lambda i, j: (i, 0)
    const2 = lambda i, j: (0, 0)
    const3 = lambda i, j: (0, 0, 0)
    mod = lambda i, j: (mod_idx(i, tm), 0, 0)
    factors = (PEER_HEADS, PEER_NKEYS, tm)
    return pl.pallas_call(
        _peer_kernel,
        grid=(t // tm, ne // ce),
        in_specs=[pl.BlockSpec((tm, d), row),
                  pl.BlockSpec((1, 1, d), mod), pl.BlockSpec((1, 1, d), mod),
                  pl.BlockSpec((1, 1, d), mod),
                  pl.BlockSpec(wq_t.shape, const2),
                  pl.BlockSpec(ka.shape, const3), pl.BlockSpec(kb.shape, const3),
                  pl.BlockSpec((ce, d), lambda i, j: (j, 0)),
                  pl.BlockSpec((d, ce), lambda i, j: (0, j)),
                  pl.BlockSpec((1, d), const2), pl.BlockSpec((1, d), const2)],
        out_specs=pl.BlockSpec((tm, d), row),
        out_shape=jax.ShapeDtypeStruct((t, d), F32),
        scratch_shapes=[pltpu.VMEM((d, tm), BF16),
                        pltpu.VMEM((2 * PEER_NKEYS, tm), F32),
                        pltpu.VMEM(factors, F32), pltpu.VMEM(factors, F32), pltpu.VMEM(factors, F32),
                        pltpu.VMEM((ce, tm), F32),
                        pltpu.VMEM((ce, tm), BF16),
                        pltpu.VMEM((d, tm), F32)],
        compiler_params=_cparams(("arbitrary", "arbitrary")),
        name="peer",
    )(h2, sh, sc, gate, wq_t, ka, kb, u_b, vt_b, ln_g.reshape(1, d), ln_b.reshape(1, d))


def _mla_proj_kernel(*refs, rotate, want_q):
    if rotate:
        (h_ref, sh_ref, sc_ref, wd_ref, qg_ref, kvg_ref, wuq_ref, wukv_ref,
         cq_ref, sq_ref, ck_ref, sk_ref) = refs[:12]
        outs = refs[12:]
    else:
        h_ref, sh_ref, sc_ref, wd_ref, qg_ref, kvg_ref, wuq_ref, wukv_ref = refs[:8]
        outs = refs[8:]
    if want_q:
        q_ref, k_ref, v_ref = outs
    else:
        k_ref, v_ref = outs
    nh = MLA_HEADS
    x = h_ref[0] * (1.0 + sc_ref[0]) + sh_ref[0]
    dn = jnp.dot(x.astype(BF16), wd_ref[...], preferred_element_type=F32)
    cq = dn[:, :MLA_Q_RANK]
    ckv = dn[:, MLA_Q_RANK:MLA_Q_RANK + MLA_KV_RANK]
    o = MLA_Q_RANK + MLA_KV_RANK
    kr = dn[:, o:o + MLA_ROPE]
    kr_sw = dn[:, o + MLA_ROPE:o + 2 * MLA_ROPE]
    if rotate:
        kr = kr * ck_ref[...] + kr_sw * sk_ref[...]
    ckvn = ckv * lax.rsqrt(jnp.mean(ckv * ckv, axis=-1, keepdims=True) + RMS_EPS) * kvg_ref[...]
    kv = jnp.dot(ckvn.astype(BF16), wukv_ref[...], preferred_element_type=F32)
    tm = x.shape[0]
    zpad = jnp.zeros((tm, MLA_QK_PAD - MLA_NOPE - MLA_ROPE), F32)
    krp = jnp.concatenate([kr, zpad], axis=-1).astype(BF16)
    for hd in range(nh):
        k_ref[0, hd, :, 0:MLA_NOPE] = kv[:, hd * MLA_NOPE:(hd + 1) * MLA_NOPE].astype(BF16)
        k_ref[0, hd, :, MLA_NOPE:] = krp
        v_ref[0, hd] = kv[:, nh * MLA_NOPE + hd * MLA_V:nh * MLA_NOPE + (hd + 1) * MLA_V].astype(BF16)
    if want_q:
        cqn = cq * lax.rsqrt(jnp.mean(cq * cq, axis=-1, keepdims=True) + RMS_EPS) * qg_ref[...]
        q = jnp.dot(cqn.astype(BF16), wuq_ref[...], preferred_element_type=F32)
        scale = (MLA_NOPE + MLA_ROPE) ** -0.5
        base = nh * MLA_NOPE
        qr = q[:, base:base + nh * MLA_ROPE]
        if rotate:
            qr_sw = q[:, base + nh * MLA_ROPE:]
            qr = qr * cq_ref[...] + qr_sw * sq_ref[...]
        for hd in range(nh):
            q_ref[0, hd, :, 0:MLA_NOPE] = (q[:, hd * MLA_NOPE:(hd + 1) * MLA_NOPE] * scale).astype(BF16)
            qrp = jnp.concatenate([qr[:, hd * MLA_ROPE:(hd + 1) * MLA_ROPE] * scale, zpad], axis=-1)
            q_ref[0, hd, :, MLA_NOPE:] = qrp.astype(BF16)


def _mla_proj(h3, sh, sc, mod_idx, wd, qg, kvg, wuq, wukv, rope, want_q):
    b, n, d = h3.shape
    tm = _tile(n, 256)
    nt = n // tm
    rotate = rope is not None
    tok = lambda i, t: (i, t, 0)
    const = lambda i, t: (0, 0)
    mod = lambda i, t: (mod_idx(i), 0, 0)
    ins = [h3, sh, sc, wd, qg, kvg, wuq, wukv]
    specs = [pl.BlockSpec((1, tm, d), tok), pl.BlockSpec((1, 1, d), mod), pl.BlockSpec((1, 1, d), mod),
             pl.BlockSpec(wd.shape, const), pl.BlockSpec(qg.shape, const), pl.BlockSpec(kvg.shape, const),
             pl.BlockSpec(wuq.shape, const), pl.BlockSpec(wukv.shape, const)]
    if rotate:
        ins += list(rope)
        specs += [pl.BlockSpec((tm, a.shape[1]), lambda i, t: (t, 0)) for a in rope]
    hd4 = lambda i, t: (i, 0, t, 0)
    out_specs, out_shape = [], []
    if want_q:
        out_specs.append(pl.BlockSpec((1, MLA_HEADS, tm, MLA_QK_PAD), hd4))
        out_shape.append(jax.ShapeDtypeStruct((b, MLA_HEADS, n, MLA_QK_PAD), BF16))
    out_specs += [pl.BlockSpec((1, MLA_HEADS, tm, MLA_QK_PAD), hd4),
                  pl.BlockSpec((1, MLA_HEADS, tm, MLA_V), hd4)]
    out_shape += [jax.ShapeDtypeStruct((b, MLA_HEADS, n, MLA_QK_PAD), BF16),
                  jax.ShapeDtypeStruct((b, MLA_HEADS, n, MLA_V), BF16)]

    return pl.pallas_call(
        functools.partial(_mla_proj_kernel, rotate=rotate, want_q=want_q),
        grid=(b, nt),
        in_specs=specs,
        out_specs=out_specs,
        out_shape=out_shape,
        compiler_params=_cparams(("arbitrary", "arbitrary")),
        name="mla_proj_q" if want_q else "mla_proj_kv",
    )(*ins)


def _mla_attn_kernel(q_ref, kc_ref, kl_ref, vc_ref, vl_ref, o_ref):
    nt = (((1,), (1,)), ((), ()))
    for hh in range(q_ref.shape[1]):
        q = q_ref[0, hh]
        s_c = lax.dot_general(q, kc_ref[0, hh], nt, preferred_element_type=F32)
        s_l = lax.dot_general(q, kl_ref[0, hh], nt, preferred_element_type=F32)
        m = jnp.maximum(jnp.max(s_c, axis=-1, keepdims=True), jnp.max(s_l, axis=-1, keepdims=True))
        p_c = jnp.exp(s_c - m)
        p_l = jnp.exp(s_l - m)
        den = jnp.sum(p_c, axis=-1, keepdims=True) + jnp.sum(p_l, axis=-1, keepdims=True)
        o = (jnp.dot(p_c.astype(BF16), vc_ref[0, hh], preferred_element_type=F32)
             + jnp.dot(p_l.astype(BF16), vl_ref[0, hh], preferred_element_type=F32))
        o_ref[0, :, hh * MLA_V:(hh + 1) * MLA_V] = (o / den).astype(o_ref.dtype)


def _mla_attn(q, kc, kl, vc, vl):
    b, nh, n, dq = q.shape
    lc = kc.shape[2]
    tq = _tile(n, 256)
    hp = 8
    kv = lambda i, h, t: (i, h, 0, 0)
    return pl.pallas_call(
        _mla_attn_kernel,
        grid=(b, nh // hp, n // tq),
        in_specs=[pl.BlockSpec((1, hp, tq, dq), lambda i, h, t: (i, h, t, 0)),
                  pl.BlockSpec((1, hp, lc, dq), kv), pl.BlockSpec((1, hp, n, dq), kv),
                  pl.BlockSpec((1, hp, lc, MLA_V), kv), pl.BlockSpec((1, hp, n, MLA_V), kv)],
        out_specs=pl.BlockSpec((1, tq, hp * MLA_V), lambda i, h, t: (i, t, h)),
        out_shape=jax.ShapeDtypeStruct((b, n, nh * MLA_V), BF16),
        compiler_params=_cparams(("arbitrary", "arbitrary", "arbitrary")),
        name="mla_attn",
    )(q, kc, kl, vc, vl)


def _rope_tables(n_tokens, reps):
    rows = n_tokens // GRID_W
    row = jnp.repeat(jnp.arange(rows, dtype=F32), GRID_W)
    col = jnp.tile(jnp.arange(GRID_W, dtype=F32), rows)
    n_freq = MLA_ROPE // 4
    inv_freq = ROPE_BASE ** (-jnp.arange(n_freq, dtype=F32) / n_freq)
    ar = row[:, None] * inv_freq
    ac = col[:, None] * inv_freq
    cos = jnp.concatenate([jnp.cos(ar), jnp.cos(ar), jnp.cos(ac), jnp.cos(ac)], axis=-1)
    sin = jnp.concatenate([-jnp.sin(ar), jnp.sin(ar), -jnp.sin(ac), jnp.sin(ac)], axis=-1)
    return jnp.tile(cos, (1, reps)), jnp.tile(sin, (1, reps))


def _swap_halves_cols(w):
    q = MLA_ROPE // 4
    return jnp.concatenate([w[..., q:2 * q], w[..., 0:q], w[..., 3 * q:4 * q], w[..., 2 * q:3 * q]], axis=-1)


def kernel(x, c, ctx, c_ctx, ada_w, ada_b, ln_tm_g, ln_tm_b, ln_cm_g, ln_cm_b, gla_w_in, gla_gate_fwd_a, gla_gate_fwd_b, gla_gate_fwd_bias, gla_gate_bwd_a, gla_gate_bwd_b, gla_gate_bwd_bias, gla_norm_g, gla_w_out, mla_w_down, mla_q_norm_g, mla_kv_norm_g, mla_w_uq, mla_w_ukv, mla_w_out, peer_w_query, peer_keys_a, peer_keys_b, peer_u, peer_v):
    b, ll, d = x.shape
    lc = ctx.shape[1]
    tl, tc = b * ll, b * lc
    assert DEPTH == ada_w.shape[0] == 2

    nb = -(-(b + 1) // SUBLANES) * SUBLANES
    c_all = jnp.concatenate([c, c_ctx[None, :], jnp.zeros((nb - b - 1, d), F32)], axis=0)
    mod = _ada(c_all, ada_w, ada_b).reshape(DEPTH, nb, 6, 1, d)

    def lat_idx(tokens_per_row):
        return lambda i, tm: (i * tm) // tokens_per_row

    lat_mod = lat_idx(ll)
    ctx_mod = lambda i, tm: b

    h_lat = x.reshape(tl, d)
    h_ctx = ctx.reshape(tc, d)

    def peer_weights(i):
        wq_t = peer_w_query[i].T.astype(BF16)
        return (wq_t, peer_keys_a[i].astype(BF16), peer_keys_b[i].astype(BF16),
                peer_u[i].astype(BF16), peer_v[i].T.astype(BF16))

    m0 = [mod[0, :, k] for k in range(6)]
    hk = gla_gate_fwd_b.shape[2]
    hv = (gla_w_in.shape[2] - 2 * hk) // 2
    w_in = gla_w_in[0].astype(BF16)
    ga = jnp.concatenate([gla_gate_fwd_a[0], gla_gate_bwd_a[0]], axis=1).astype(BF16)
    zr = jnp.zeros((GLA_GATE_RANK, hk), F32)
    gb = jnp.concatenate([jnp.concatenate([gla_gate_fwd_b[0], zr], axis=1),
                          jnp.concatenate([zr, gla_gate_bwd_b[0]], axis=1)], axis=0).astype(BF16)
    gbias = jnp.concatenate([gla_gate_fwd_bias[0], gla_gate_bwd_bias[0]])[None, :]
    proj_l = _gla_inproj(h_lat, m0[0], m0[1], lat_mod, w_in, ga, gb, gbias, hk, hv)
    proj_c = _gla_inproj(h_ctx, m0[0], m0[1], ctx_mod, w_in, ga, gb, gbias, hk, hv)
    on_c, on_l = _gla_scan(proj_c, proj_l, gla_norm_g[0], b, lc, ll, hk, hv)
    w_out = gla_w_out[0].astype(BF16)
    h_lat = _mix_out(on_l.reshape(tl, hv), proj_l[3], h_lat, m0[2], lat_mod, w_out, ln_tm_g[0], ln_tm_b[0])
    h_ctx = _mix_out(on_c.reshape(tc, hv), proj_c[3], h_ctx, m0[2], ctx_mod, w_out, ln_tm_g[0], ln_tm_b[0])
    pw = peer_weights(0)
    h_lat = _peer(h_lat, m0[3], m0[4], m0[5], lat_mod, *pw, ln_cm_g[0], ln_cm_b[0])
    h_ctx = _peer(h_ctx, m0[3], m0[4], m0[5], ctx_mod, *pw, ln_cm_g[0], ln_cm_b[0])

    m1 = [mod[1, :, k] for k in range(6)]
    nh = MLA_HEADS
    wd = mla_w_down[0]
    o = MLA_Q_RANK + MLA_KV_RANK
    wd = jnp.concatenate([wd, _swap_halves_cols(wd[:, o:])], axis=1).astype(BF16)
    wuq = mla_w_uq[0].reshape(MLA_Q_RANK, nh, MLA_NOPE + MLA_ROPE)
    wuq_rope = wuq[:, :, MLA_NOPE:]
    wuq = jnp.concatenate([wuq[:, :, :MLA_NOPE].reshape(MLA_Q_RANK, -1),
                           wuq_rope.reshape(MLA_Q_RANK, -1),
                           _swap_halves_cols(wuq_rope).reshape(MLA_Q_RANK, -1)], axis=1).astype(BF16)
    wukv = mla_w_ukv[0].reshape(MLA_KV_RANK, nh, MLA_NOPE + MLA_V)
    wukv = jnp.concatenate([wukv[:, :, :MLA_NOPE].reshape(MLA_KV_RANK, -1),
                            wukv[:, :, MLA_NOPE:].reshape(MLA_KV_RANK, -1)], axis=1).astype(BF16)
    qg = mla_q_norm_g[0][None, :]
    kvg = mla_kv_norm_g[0][None, :]
    cos_q, sin_q = _rope_tables(ll, nh)
    cos_k, sin_k = _rope_tables(ll, 1)
    q_l, k_l, v_l = _mla_proj(h_lat.reshape(b, ll, d), m1[0], m1[1], lambda i: i, wd, qg, kvg, wuq, wukv,
                              (cos_q, sin_q, cos_k, sin_k), True)
    k_c, v_c = _mla_proj(h_ctx.reshape(b, lc, d), m1[0], m1[1], lambda i: b, wd, qg, kvg, wuq, wukv,
                         None, False)
    o_l = _mla_attn(q_l, k_c, k_l, v_c, v_l)
    h_lat = _mix_out(o_l.reshape(tl, nh * MLA_V), None, h_lat, m1[2], lat_mod, mla_w_out[0].astype(BF16),
                     ln_tm_g[1], ln_tm_b[1])
    pw = peer_weights(1)
    h_lat = _peer(h_lat, m1[3], m1[4], m1[5], lat_mod, *pw, ln_cm_g[1], ln_cm_b[1])
    return h_lat.reshape(b, ll, d)
```
